```python
import math, functools
import jax, jax.numpy as jnp
from jax import lax
import numpy as np

D_MODEL = 4096
BATCH = 4
SEQ = 2048
DEPTH = 2
DEC_BATCH = 8
DEC_SEQ = 4
PAST_LEN = 16384
PAGE_SIZE = 128

HEAD_DIM = 128
N_MIX_HEADS = D_MODEL // HEAD_DIM
N_ATTN_HEADS = (3 * N_MIX_HEADS) // 8
N_DELTA_HEADS = (3 * N_MIX_HEADS) // 8
D_ATTN = N_ATTN_HEADS * HEAD_DIM
D_DELTA = N_DELTA_HEADS * HEAD_DIM
D_POOL = D_MODEL - D_ATTN - D_DELTA
QK_HALF = HEAD_DIM // 2
POOL_WINDOWS = (2, 4, 8, 16)
POOL_GROUP = D_POOL // 4
POOL_BUF = 15
CONV_W = 4
DELTA_CHUNK = 64
N_META = 16
N_GROUPS = 4
EXPERTS_PER_GROUP = 8
N_EXPERTS = N_GROUPS * EXPERTS_PER_GROUP
TOP_K = 2
D_EXPERT = D_MODEL // 4
MOE_BLOCK = 128
Q_BLOCK = 128
RMS_EPS = 1e-6
NEG_INF = -1e30
D_IN = 3 * D_ATTN + D_POOL + 4 * D_DELTA + 2 * N_DELTA_HEADS

kernel_name = 'hymba_diffattn_pool_gdn_hiermoe_step'


def rms_norm(x, gain):
    xf = x.astype(jnp.float32)
    y = xf * lax.rsqrt(jnp.mean(xf * xf, axis=-1, keepdims=True) + RMS_EPS)
    return (y * gain.astype(jnp.float32)).astype(x.dtype)


def l2_normalize(x):
    return x * lax.rsqrt(jnp.sum(x * x, axis=-1, keepdims=True) + RMS_EPS)


def alibi_slopes(n_heads):
    def pow2_slopes(m):
        start = 2.0 ** (-8.0 / m)
        return [start ** (i + 1) for i in range(m)]
    closest = 2 ** int(math.floor(math.log2(n_heads)))
    slopes = pow2_slopes(closest) + pow2_slopes(2 * closest)[0::2][: n_heads - closest]
    return jnp.asarray(np.array(slopes, dtype=np.float32))


def diff_attend_core(q, k, v, qpos, kpos, lam, slopes):
    q1, q2 = q[..., :QK_HALF], q[..., QK_HALF:]
    k1, k2 = k[..., :QK_HALF], k[..., QK_HALF:]
    dist = qpos[:, None] - kpos[None, :]
    causal = dist >= 0
    bias = -slopes[:, None, None] * jnp.abs(dist).astype(jnp.float32)[None]

    def probs(qh, kh):
        s = jnp.einsum('bqhd,bkhd->bhqk', qh, kh).astype(jnp.float32) * (QK_HALF ** -0.5) + bias
        return jax.nn.softmax(jnp.where(causal, s, NEG_INF), axis=-1)

    pmap = probs(q1, k1) - lam * probs(q2, k2)
    return jnp.einsum('bhqk,bkhd->bqhd', pmap.astype(v.dtype), v)


def prompt_attend(q, k, v, lam, slopes):
    b, t, h, d = q.shape
    nb = -(-t // Q_BLOCK)
    pad = nb * Q_BLOCK - t
    qb = jnp.pad(q, ((0, 0), (0, pad), (0, 0), (0, 0))).reshape(b, nb, Q_BLOCK, h, d).swapaxes(0, 1)
    kpos = jnp.arange(t)

    def block(args):
        qi, start = args
        return diff_attend_core(qi, k, v, start + jnp.arange(Q_BLOCK), kpos, lam, slopes)

    o = lax.map(block, (qb, jnp.arange(nb) * Q_BLOCK))
    return o.swapaxes(0, 1).reshape(b, nb * Q_BLOCK, h, d)[:, :t]


def paged_attend(q, k_new, v_new, lam, slopes, cache_k, cache_v, layer, page_table):
    n_seq, n_pages = page_table.shape
    past = n_pages * PAGE_SIZE
    k_past = cache_k[layer, page_table].reshape(n_seq, past, N_ATTN_HEADS, HEAD_DIM)
    v_past = cache_v[layer, page_table].reshape(n_seq, past, N_ATTN_HEADS, HEAD_DIM)
    k_all = jnp.concatenate([k_past.astype(k_new.dtype), k_new], axis=1)
    v_all = jnp.concatenate([v_past.astype(v_new.dtype), v_new], axis=1)
    t = q.shape[1]
    return diff_attend_core(q, k_all, v_all, past + jnp.arange(t), jnp.arange(past + t), lam, slopes)


def pool_mix(u, prefix, pos0, w_pool, scale):
    t = u.shape[1]
    ue = jnp.concatenate([prefix.astype(u.dtype), u], axis=1)
    uf = ue.astype(jnp.float32)
    cs = jnp.pad(jnp.cumsum(uf, axis=1), ((0, 0), (1, 0), (0, 0)))
    pos = pos0 + jnp.arange(t)
    outs = []
    for gi, w in enumerate(POOL_WINDOWS):
        ch = slice(gi * POOL_GROUP, (gi + 1) * POOL_GROUP)
        window_sum = cs[:, POOL_BUF + 1:POOL_BUF + 1 + t, ch] - cs[:, POOL_BUF + 1 - w:POOL_BUF + 1 - w + t, ch]
        count = jnp.minimum(pos + 1, w).astype(jnp.float32)
        diff = window_sum / count[None, :, None] - uf[:, POOL_BUF:, ch]
        outs.append(jnp.einsum('btc,cd->btd', diff.astype(u.dtype), w_pool[gi]))
    out = jnp.concatenate(outs, axis=-1) * scale
    return out.astype(u.dtype), ue[:, -POOL_BUF:]


def gated_delta_chunked(q, k, v, g, beta, s0, chunk):
    b, t, h, _ = q.shape
    dv = v.shape[-1]
    pad = (-t) % chunk
    if pad:
        p4 = ((0, 0), (0, pad), (0, 0), (0, 0))
        p3 = ((0, 0), (0, pad), (0, 0))
        q, k, v = jnp.pad(q, p4), jnp.pad(k, p4), jnp.pad(v, p4)
        g, beta = jnp.pad(g, p3), jnp.pad(beta, p3)
    nc = (t + pad) // chunk

    def blocks(a):
        return a.reshape(b, nc, chunk, h, -1).transpose(1, 0, 3, 2, 4)

    qc, kc, vc = blocks(q), blocks(k), blocks(v)
    gc = g.reshape(b, nc, chunk, h).transpose(1, 0, 3, 2)
    bc = beta.reshape(b, nc, chunk, h).transpose(1, 0, 3, 2)
    cum = jnp.cumsum(gc, axis=-1)
    idx = jnp.arange(chunk)
    incl = idx[:, None] >= idx[None, :]
    strict = idx[:, None] > idx[None, :]
    decay = jnp.exp(jnp.where(incl, cum[..., :, None] - cum[..., None, :], NEG_INF))
    kk = jnp.einsum('nbhid,nbhjd->nbhij', kc, kc)
    lower = jnp.where(strict, bc[..., :, None] * kk * decay, 0.0)
    rhs = jnp.concatenate([vc * bc[..., None], kc * (bc * jnp.exp(cum))[..., None]], axis=-1)
    sol = lax.linalg.triangular_solve(lower, rhs, left_side=True, lower=True, unit_diagonal=True)
    new_v, k_cum = sol[..., :dv], sol[..., dv:]
    qk = jnp.einsum('nbhid,nbhjd->nbhij', qc, kc) * decay
    q_dec = qc * jnp.exp(cum)[..., None]
    k_dec = kc * jnp.exp(cum[..., -1:] - cum)[..., None]
    g_tot = jnp.exp(cum[..., -1])

    def step(state, xs):
        nv, kcu, qkc, qd, kd, gt = xs
        w = nv - jnp.einsum('bhcd,bhde->bhce', kcu, state)
        out = jnp.einsum('bhcd,bhde->bhce', qd, state) + jnp.einsum('bhij,bhje->bhie', qkc, w)
        state = state * gt[..., None, None] + jnp.einsum('bhcd,bhce->bhde', kd, w)
        return state, out

    s_final, o = lax.scan(step, s0, (new_v, k_cum, qk, q_dec, k_dec, g_tot))
    o = o.transpose(1, 0, 3, 2, 4).reshape(b, nc * chunk, h, dv)[:, :t]
    return o, s_final


def gated_delta_mixer(qkv, z, b_raw, a_raw, conv_prefix, s0, segments, lw):
    b, t, _ = qkv.shape
    xe = jnp.concatenate([conv_prefix.astype(qkv.dtype), qkv], axis=1)
    conv = xe[:, 0:t] * lw['conv_w'][0]
    for i in range(1, CONV_W):
        conv = conv + xe[:, i:i + t] * lw['conv_w'][i]
    conv = jax.nn.silu(conv.astype(jnp.float32))
    q, k, v = jnp.split(conv, 3, axis=-1)
    heads = lambda a: a.reshape(b, t, N_DELTA_HEADS, HEAD_DIM)
    q = l2_normalize(heads(q)) * (HEAD_DIM ** -0.5)
    k = l2_normalize(heads(k))
    v = heads(v)
    beta = jax.nn.sigmoid(b_raw.astype(jnp.float32))
    g = -jnp.exp(lw['a_log'].astype(jnp.float32)) * jax.nn.softplus(a_raw.astype(jnp.float32) + lw['dt_bias'].astype(jnp.float32))
    state = s0.astype(jnp.float32)
    outs = []
    for start, stop, chunk in segments:
        o_seg, state = gated_delta_chunked(q[:, start:stop], k[:, start:stop], v[:, start:stop],
                                           g[:, start:stop], beta[:, start:stop], state, chunk)
        outs.append(o_seg)
    o = jnp.concatenate(outs, axis=1)
    o = rms_norm(o, lw['delta_norm']) * jax.nn.silu(heads(z).astype(jnp.float32))
    return o.reshape(b, t, D_DELTA).astype(qkv.dtype), xe[:, t:], state.astype(s0.dtype)


def mixing_sublayer(xn, lw, lam_init, attend, pool_prefix, pos0, conv_prefix, s0, segments):
    b, t, _ = xn.shape
    proj = jnp.einsum('btd,de->bte', xn, lw['w_in'])
    sizes = (D_ATTN, D_ATTN, D_ATTN, D_POOL, 3 * D_DELTA, D_DELTA, N_DELTA_HEADS, N_DELTA_HEADS)
    cuts = [sum(sizes[:i + 1]) for i in range(len(sizes) - 1)]
    q_a, k_a, v_a, u, qkv_c, z_c, b_c, a_c = jnp.split(proj, cuts, axis=-1)
    heads = lambda a: a.reshape(b, t, N_ATTN_HEADS, HEAD_DIM)
    q_a, k_a, v_a = heads(q_a), heads(k_a), heads(v_a)
    lam = (jnp.exp(jnp.sum(lw['lambda_q1'].astype(jnp.float32) * lw['lambda_k1'].astype(jnp.float32)))
           - jnp.exp(jnp.sum(lw['lambda_q2'].astype(jnp.float32) * lw['lambda_k2'].astype(jnp.float32)))
           + lam_init)
    o_a = attend(q_a, k_a, v_a, lam)
    o_a = rms_norm(o_a, lw['subln_gain']) * (1.0 - lam_init)
    o_b, pool_new = pool_mix(u, pool_prefix, pos0, lw['w_pool'], lw['pool_scale'])
    o_c, conv_new, s_new = gated_delta_mixer(qkv_c, z_c, b_c, a_c, conv_prefix, s0, segments, lw)
    o = jnp.concatenate([o_a.reshape(b, t, D_ATTN).astype(xn.dtype), o_b, o_c], axis=-1)
    return jnp.einsum('bte,ed->btd', o, lw['w_out']), (k_a, v_a, pool_new, conv_new, s_new)


def routed_experts(xf, experts, gates, w_gate, w_up, w_down):
    n, d = xf.shape
    a = n * TOP_K
    flat_e = experts.reshape(-1)
    order = jnp.argsort(flat_e)
    e_sorted = flat_e[order]
    counts = jnp.bincount(flat_e, length=N_EXPERTS)
    padded = (counts + MOE_BLOCK - 1) // MOE_BLOCK * MOE_BLOCK
    pad_end = jnp.cumsum(padded)
    pad_start = pad_end - padded
    start = jnp.cumsum(counts) - counts
    dest = pad_start[e_sorted] + jnp.arange(a) - start[e_sorted]
    n_blocks = -(-a // MOE_BLOCK) + N_EXPERTS
    tok = order // TOP_K
    buf = jnp.zeros((n_blocks * MOE_BLOCK, d), xf.dtype).at[dest].set(xf[tok])
    block_e = jnp.minimum(jnp.searchsorted(pad_end, jnp.arange(n_blocks) * MOE_BLOCK, side='right'), N_EXPERTS - 1)

    def run(args):
        xb, e = args
        hdn = jax.nn.silu(xb @ w_gate[e]) * (xb @ w_up[e])
        return hdn @ w_down[e]

    yb = lax.map(run, (buf.reshape(n_blocks, MOE_BLOCK, d), block_e)).reshape(-1, d)
    rows = yb[dest] * gates.reshape(-1)[order][:, None].astype(xf.dtype)
    return jnp.zeros_like(xf).at[tok].add(rows)


def hierarchical_moe(h, w_rg, w_re, w_gate, w_up, w_down):
    b, t, d = h.shape
    xf = h.reshape(-1, d)
    lg = jnp.einsum('nd,dg->ng', xf, w_rg).astype(jnp.float32)
    pg = jax.nn.softmax(lg, axis=-1)
    g_star = jnp.argmax(lg, axis=-1)
    p_sel = jnp.take_along_axis(pg, g_star[:, None], axis=-1)
    le = jnp.einsum('nd,gde->nge', xf, w_re).astype(jnp.float32)
    le = jnp.take_along_axis(le, g_star[:, None, None], axis=1)[:, 0]
    top_p, top_i = lax.top_k(jax.nn.softmax(le, axis=-1), TOP_K)
    gates = p_sel * top_p / jnp.sum(top_p, axis=-1, keepdims=True)
    experts = g_star[:, None] * EXPERTS_PER_GROUP + top_i
    return routed_experts(xf, experts, gates, w_gate, w_up, w_down).reshape(b, t, d)


def setup_inputs(seed: int = 0) -> dict:
    key = jax.random.key(seed)
    ks = iter(jax.random.split(key, 40))
    f32 = jnp.float32
    n_pages = PAST_LEN // PAGE_SIZE
    n_pool = (DEC_BATCH * n_pages * 5) // 4

    def normal(shape, scale=1.0):
        return jax.random.normal(next(ks), shape, f32) * scale

    x_prompt = normal((BATCH, SEQ, D_MODEL))
    x_sample = normal((DEC_BATCH, DEC_SEQ, D_MODEL))
    cache_k = normal((DEPTH, n_pool, PAGE_SIZE, N_ATTN_HEADS, HEAD_DIM))
    cache_v = normal((DEPTH, n_pool, PAGE_SIZE, N_ATTN_HEADS, HEAD_DIM))
    state_pool = normal((DEPTH, DEC_BATCH, POOL_BUF, D_POOL))
    state_conv = normal((DEPTH, DEC_BATCH, CONV_W - 1, 3 * D_DELTA))
    state_delta = normal((DEPTH, DEC_BATCH, N_DELTA_HEADS, HEAD_DIM, HEAD_DIM), 0.1)
    perm = jax.random.permutation(next(ks), n_pool)
    page_table = perm[: DEC_BATCH * n_pages].reshape(DEC_BATCH, n_pages).astype(jnp.int32)
    a_init = jax.random.uniform(next(ks), (DEPTH, N_DELTA_HEADS), f32, 1.0, 16.0)
    dt = jnp.exp(jax.random.uniform(next(ks), (DEPTH, N_DELTA_HEADS), f32, math.log(1e-3), math.log(1e-1)))
    return {
        'x_prompt': x_prompt, 'x_sample': x_sample,
        'cache_k': cache_k, 'cache_v': cache_v,
        'state_pool': state_pool, 'state_conv': state_conv, 'state_delta': state_delta,
        'page_table': page_table,
        'meta': normal((N_META, D_MODEL)),
        'norm_mix': 1.0 + normal((DEPTH, D_MODEL), 0.02),
        'w_in': normal((DEPTH, D_MODEL, D_IN), D_MODEL ** -0.5),
        'lambda_q1': normal((DEPTH, QK_HALF), 0.1),
        'lambda_k1': normal((DEPTH, QK_HALF), 0.1),
        'lambda_q2': normal((DEPTH, QK_HALF), 0.1),
        'lambda_k2': normal((DEPTH, QK_HALF), 0.1),
        'subln_gain': 1.0 + normal((DEPTH, HEAD_DIM), 0.02),
        'w_pool': normal((DEPTH, len(POOL_WINDOWS), POOL_GROUP, POOL_GROUP), POOL_GROUP ** -0.5),
        'pool_scale': 1.0 + normal((DEPTH, D_POOL), 0.1),
        'conv_w': normal((DEPTH, CONV_W, 3 * D_DELTA), CONV_W ** -0.5),
        'a_log': jnp.log(a_init),
        'dt_bias': dt + jnp.log(-jnp.expm1(-dt)),
        'delta_norm': 1.0 + normal((DEPTH, HEAD_DIM), 0.02),
        'w_out': normal((DEPTH, D_MODEL, D_MODEL), D_MODEL ** -0.5),
        'norm_ffn': 1.0 + normal((DEPTH, D_MODEL), 0.02),
        'w_router_group': normal((DEPTH, D_MODEL, N_GROUPS), D_MODEL ** -0.5),
        'w_router_expert': normal((DEPTH, N_GROUPS, D_MODEL, EXPERTS_PER_GROUP), D_MODEL ** -0.5),
        'w_gate': normal((DEPTH, N_EXPERTS, D_MODEL, D_EXPERT), D_MODEL ** -0.5),
        'w_up': normal((DEPTH, N_EXPERTS, D_MODEL, D_EXPERT), D_MODEL ** -0.5),
        'w_down': normal((DEPTH, N_EXPERTS, D_EXPERT, D_MODEL), D_EXPERT ** -0.5),
        'norm_final': 1.0 + normal((D_MODEL,), 0.02),
    }


def reference(x_prompt, x_sample, cache_k, cache_v, state_pool, state_conv, state_delta, page_table,
              meta, norm_mix, w_in, lambda_q1, lambda_k1, lambda_q2, lambda_k2, subln_gain,
              w_pool, pool_scale, conv_w, a_log, dt_bias, delta_norm, w_out, norm_ffn,
              w_router_group, w_router_expert, w_gate, w_up, w_down, norm_final):
    slopes = alibi_slopes(N_ATTN_HEADS)
    n_b = x_prompt.shape[0]
    hp = jnp.concatenate([jnp.broadcast_to(meta.astype(x_prompt.dtype)[None], (n_b, N_META, D_MODEL)), x_prompt], axis=1)
    hs = x_sample
    tp = hp.shape[1]
    ts = hs.shape[1]
    past = page_table.shape[1] * PAGE_SIZE
    k_p, v_p, pool_p, conv_p, delta_p = [], [], [], [], []
    k_s, v_s, pool_s, conv_s, delta_s = [], [], [], [], []
    for l in range(DEPTH):
        lw = {'w_in': w_in[l], 'lambda_q1': lambda_q1[l], 'lambda_k1': lambda_k1[l],
              'lambda_q2': lambda_q2[l], 'lambda_k2': lambda_k2[l], 'subln_gain': subln_gain[l],
              'w_pool': w_pool[l], 'pool_scale': pool_scale[l], 'conv_w': conv_w[l],
              'a_log': a_log[l], 'dt_bias': dt_bias[l], 'delta_norm': delta_norm[l], 'w_out': w_out[l]}
        lam_init = 0.8 - 0.6 * math.exp(-0.3 * l)
        mix, (kn, vn, pn, cn, sn) = mixing_sublayer(
            rms_norm(hp, norm_mix[l]), lw, lam_init,
            functools.partial(prompt_attend, slopes=slopes),
            jnp.zeros((n_b, POOL_BUF, D_POOL), hp.dtype), 0,
            jnp.zeros((n_b, CONV_W - 1, 3 * D_DELTA), hp.dtype),
            jnp.zeros((n_b, N_DELTA_HEADS, HEAD_DIM, HEAD_DIM), state_delta.dtype),
            ((0, N_META, N_META), (N_META, tp, DELTA_CHUNK)))
        hp = hp + mix
        hp = hp + hierarchical_moe(rms_norm(hp, norm_ffn[l]), w_router_group[l], w_router_expert[l],
                                   w_gate[l], w_up[l], w_down[l])
        k_p.append(kn); v_p.append(vn); pool_p.append(pn); conv_p.append(cn); delta_p.append(sn)
        mix, (kn, vn, pn, cn, sn) = mixing_sublayer(
            rms_norm(hs, norm_mix[l]), lw, lam_init,
            functools.partial(paged_attend, slopes=slopes, cache_k=cache_k, cache_v=cache_v,
                              layer=l, page_table=page_table),
            state_pool[l], past, state_conv[l], state_delta[l],
            ((0, ts, min(DELTA_CHUNK, ts)),))
        hs = hs + mix
        hs = hs + hierarchical_moe(rms_norm(hs, norm_ffn[l]), w_router_group[l], w_router_expert[l],
                                   w_gate[l], w_up[l], w_down[l])
        k_s.append(kn); v_s.append(vn); pool_s.append(pn); conv_s.append(cn); delta_s.append(sn)
    y_prompt = rms_norm(hp, norm_final)[:, N_META:]
    y_sample = rms_norm(hs, norm_final)
    return (y_prompt, y_sample,
            jnp.stack(k_p), jnp.stack(v_p), jnp.stack(pool_p), jnp.stack(conv_p), jnp.stack(delta_p),
            jnp.stack(k_s), jnp.stack(v_s), jnp.stack(pool_s), jnp.stack(conv_s), jnp.stack(delta_s))
```

```python
import functools
import math

import numpy as np
import jax
import jax.numpy as jnp
from jax import lax
from jax.experimental import pallas as pl
from jax.experimental.pallas import tpu as pltpu

F32 = jnp.float32
BF16 = jnp.bfloat16
I32 = jnp.int32

HEAD_DIM = 128
QK_HALF = HEAD_DIM // 2
POOL_WINDOWS = (2, 4, 8, 16)
POOL_BUF = 15
CONV_W = 4
N_META = 16
N_GROUPS = 4
EXPERTS_PER_GROUP = 8
N_EXPERTS = N_GROUPS * EXPERTS_PER_GROUP
TOP_K = 2
PAGE_SIZE = 128
RMS_EPS = 1e-6
NEG_INF = -1e30

LANES = 128
SUBLANES = 8
MIB = 1024 * 1024

PROMPT_ROW_TILE = 688
PROMPT_TOKEN_ROWS = 344
PROJ_COL_TILE = 512
ATTN_BLOCK = 256
DELTA_CHUNK = 48
DELTA_BASE = 8
DELTA_HEADS_PER_STEP = 4
MOE_ROWS = 512
MOE_SUB = 128
MOE_HID_TILE = 256
MOE_OUT_TILE = 1024
GATHER_ROWS = 256


def _cparams(semantics, vmem_mib):
    return pltpu.CompilerParams(dimension_semantics=semantics, vmem_limit_bytes=vmem_mib * MIB)


def _alibi_slopes(n_heads):
    def pow2_slopes(m):
        start = 2.0 ** (-8.0 / m)
        return [start ** (i + 1) for i in range(m)]
    closest = 2 ** int(math.floor(math.log2(n_heads)))
    slopes = pow2_slopes(closest) + pow2_slopes(2 * closest)[0::2][: n_heads - closest]
    return np.array(slopes, dtype=np.float32)


def _sigmoid(x):
    return 1.0 / (1.0 + jnp.exp(-x))


def _silu(x):
    return x * _sigmoid(x)


def _split_bf16(x):
    hi = x.astype(BF16)
    lo = (x - hi.astype(F32)).astype(BF16)
    return hi, lo


def _stack_hilo(x):
    hi = x.astype(BF16).astype(F32)
    return jnp.concatenate([hi, x - hi], axis=0).astype(BF16)


def _dot(a, b):
    return jnp.dot(a, b, preferred_element_type=F32)


def _dot_nt(a, b):
    return lax.dot_general(a, b, (((1,), (1,)), ((), ())), preferred_element_type=F32)


def _dot_tn(a, b):
    return lax.dot_general(a, b, (((0,), (0,)), ((), ())), preferred_element_type=F32)


def _dot3(a, b, dot=_dot):
    ah, al = _split_bf16(a)
    bh, bl = _split_bf16(b)
    return dot(ah, bh) + (dot(ah, bl) + dot(al, bh))


def _proj_kernel(*refs, normed, residual, precise):
    it = iter(refs)
    x_ref = next(it)
    g_ref = next(it) if normed else None
    w_ref = next(it)
    r_ref = next(it) if residual else None
    o_ref = next(it)
    xh_ref = next(it)
    xl_ref = next(it) if precise else None

    @pl.when(pl.program_id(1) == 0)
    def _():
        x = x_ref[...]
        if normed:
            ms = jnp.mean(x * x, axis=-1, keepdims=True)
            x = x * lax.rsqrt(ms + RMS_EPS) * g_ref[...]
        if precise:
            xh_ref[...], xl_ref[...] = _split_bf16(x)
        else:
            xh_ref[...] = x.astype(BF16)

    if precise:
        wh, wl = _split_bf16(w_ref[...])
        acc = _dot(xh_ref[...], wh) + (_dot(xh_ref[...], wl) + _dot(xl_ref[...], wh))
    else:
        acc = _dot(xh_ref[...], w_ref[...])
    if residual:
        acc = r_ref[...] + acc
    o_ref[...] = acc


def _dense(x, w, gain=None, res=None, *, tm, tn=PROJ_COL_TILE, precise=False):
    m, d = x.shape
    n = w.shape[1]
    n_out = pl.cdiv(n, tn) * tn
    assert m % tm == 0
    in_specs = [pl.BlockSpec((tm, d), lambda i, j: (i, 0))]
    args = [x]
    if gain is not None:
        in_specs.append(pl.BlockSpec((1, d), lambda i, j: (0, 0)))
        args.append(gain.reshape(1, d))
    in_specs.append(pl.BlockSpec((d, tn), lambda i, j: (0, j)))
    args.append(w)
    if res is not None:
        in_specs.append(pl.BlockSpec((tm, tn), lambda i, j: (i, j)))
        args.append(res)
    return pl.pallas_call(
        functools.partial(_proj_kernel, normed=gain is not None, residual=res is not None, precise=precise),
        grid=(m // tm, n_out // tn),
        in_specs=in_specs,
        out_specs=pl.BlockSpec((tm, tn), lambda i, j: (i, j)),
        out_shape=jax.ShapeDtypeStruct((m, n_out), F32),
        scratch_shapes=[pltpu.VMEM((tm, d), BF16)] * (2 if precise else 1),
        compiler_params=_cparams(("parallel", "arbitrary"), 56),
    )(*args)


def _lambda_value(lq1, lk1, lq2, lk2, lam_init):
    a = jnp.sum(lq1[...] * lk1[...], axis=-1, keepdims=True)
    b = jnp.sum(lq2[...] * lk2[...], axis=-1, keepdims=True)
    return jnp.exp(a) - jnp.exp(b) + lam_init


def _subln(o, gain, lam_init):
    o = o * lax.rsqrt(jnp.mean(o * o, axis=-1, keepdims=True) + RMS_EPS) * gain
    return o * (1.0 - lam_init)


def _prompt_attn_kernel(slopes_ref, lq1, lk1, lq2, lk2, gain_ref, q_ref, k_ref, v_ref, o_ref,
                        qp_ref, kp_ref, vp_ref, op_ref, *, t, tb, lam_init):
    h = pl.program_id(1)
    slope = slopes_ref[h]
    lam = _lambda_value(lq1, lk1, lq2, lk2, lam_init)
    tp = qp_ref.shape[0]
    nb = tp // tb
    for src, dst in ((q_ref, qp_ref), (k_ref, kp_ref), (v_ref, vp_ref)):
        dst[pl.ds(0, t), :] = src[...]
        dst[pl.ds(t, tp - t), :] = jnp.zeros((tp - t, HEAD_DIM), F32)
    dloc = lax.broadcasted_iota(I32, (tb, tb), 0) - lax.broadcasted_iota(I32, (tb, tb), 1)

    def qblock(qi, carry):
        r0 = pl.multiple_of(qi * tb, tb)
        q = qp_ref[pl.ds(r0, tb), :] * (QK_HALF ** -0.5)
        q1 = q[:, :QK_HALF].astype(BF16)
        q2 = q[:, QK_HALF:].astype(BF16)

        def kvblock(kj, c):
            m1, l1, a1, m2, l2, a2 = c
            c0 = pl.multiple_of(kj * tb, tb)
            kk = kp_ref[pl.ds(c0, tb), :]
            vv = vp_ref[pl.ds(c0, tb), :].astype(BF16)
            dist = dloc + (r0 - c0)
            causal = dist >= 0
            bias = slope * dist.astype(F32)

            def upd(qh, kh, m, l, a):
                s = _dot_nt(qh, kh)
                s = jnp.where(causal, s - bias, NEG_INF)
                mn = jnp.maximum(m, jnp.max(s, axis=-1, keepdims=True))
                p = jnp.exp(s - mn)
                al = jnp.exp(m - mn)
                l = al * l + jnp.sum(p, axis=-1, keepdims=True)
                a = al * a + _dot(p.astype(BF16), vv)
                return mn, l, a

            m1, l1, a1 = upd(q1, kk[:, :QK_HALF].astype(BF16), m1, l1, a1)
            m2, l2, a2 = upd(q2, kk[:, QK_HALF:].astype(BF16), m2, l2, a2)
            return m1, l1, a1, m2, l2, a2

        m0 = jnp.full((tb, 1), NEG_INF, F32)
        z1 = jnp.zeros((tb, 1), F32)
        za = jnp.zeros((tb, HEAD_DIM), F32)
        m1, l1, a1, m2, l2, a2 = lax.fori_loop(0, qi + 1, kvblock, (m0, z1, za, m0, z1, za))
        o = a1 / l1 - lam * (a2 / l2)
        op_ref[pl.ds(r0, tb), :] = _subln(o, gain_ref[...], lam_init)
        return carry

    lax.fori_loop(0, nb, qblock, 0)
    o_ref[...] = op_ref[pl.ds(0, t), :]


def _prompt_attention(proj, lw, lam_init, slopes, n_b, t, d_attn):
    n_h = d_attn // HEAD_DIM
    tb = ATTN_BLOCK
    tp = -(-t // tb) * tb
    vec = lambda n: pl.BlockSpec((1, n), lambda b, h: (0, 0))
    blk = lambda off: pl.BlockSpec((t, HEAD_DIM), lambda b, h, off=off: (b, off + h))
    return pl.pallas_call(
        functools.partial(_prompt_attn_kernel, t=t, tb=tb, lam_init=lam_init),
        grid=(n_b, n_h),
        in_specs=[pl.BlockSpec(memory_space=pltpu.SMEM),
                  vec(QK_HALF), vec(QK_HALF), vec(QK_HALF), vec(QK_HALF), vec(HEAD_DIM),
                  blk(0), blk(n_h), blk(2 * n_h)],
        out_specs=pl.BlockSpec((t, HEAD_DIM), lambda b, h: (b, h)),
        out_shape=jax.ShapeDtypeStruct((n_b * t, d_attn), F32),
        scratch_shapes=[pltpu.VMEM((tp, HEAD_DIM), F32)] * 4,
        compiler_params=_cparams(("parallel", "parallel"), 40),
    )(slopes, lw['lambda_q1'].reshape(1, -1), lw['lambda_k1'].reshape(1, -1),
      lw['lambda_q2'].reshape(1, -1), lw['lambda_k2'].reshape(1, -1), lw['subln_gain'].reshape(1, -1),
      proj, proj, proj)


def _paged_attn_kernel(pt_ref, lq1, lk1, lq2, lk2, gain_ref, qkv_ref, kc_ref, vc_ref, o_ref,
                       qm_ref, m_ref, l_ref, acc_ref, *, ts, n_h, n_pages, lam_init, slopes):
    p = pl.program_id(1)
    d_attn = n_h * HEAD_DIM
    past = n_pages * PAGE_SIZE
    rows2 = 2 * ts

    @pl.when(p == 0)
    def _():
        lane = lax.broadcasted_iota(I32, (rows2, HEAD_DIM), 1)
        srow = lax.broadcasted_iota(I32, (rows2, HEAD_DIM), 0)
        for h in range(n_h):
            q = qkv_ref[:, pl.ds(h * HEAD_DIM, HEAD_DIM)] * (QK_HALF ** -0.5)
            qa = jnp.where(lane < QK_HALF, q, 0.0)
            qb = jnp.where(lane >= QK_HALF, pltpu.roll(q, ts, 0), 0.0)
            qm_ref[h] = jnp.where(srow < ts, qa, qb)
        m_ref[...] = jnp.full(m_ref.shape, NEG_INF, F32)
        l_ref[...] = jnp.zeros(l_ref.shape, F32)
        acc_ref[...] = jnp.zeros(acc_ref.shape, F32)

    def stacked3(lhs, rhs, dot):
        rh, rl = _split_bf16(rhs)
        both = dot(_stack_hilo(lhs), rh)
        return both[:rows2] + (both[rows2:] + dot(lhs.astype(BF16), rl))

    def update(h, s, v):
        m_old = m_ref[h]
        mn = jnp.maximum(m_old, jnp.max(s, axis=-1, keepdims=True))
        pr = jnp.exp(s - mn)
        al = jnp.exp(m_old - mn)
        l_ref[h] = al * l_ref[h] + jnp.sum(pr, axis=-1, keepdims=True)
        acc_ref[h] = al * acc_ref[h] + stacked3(pr, v, _dot)
        m_ref[h] = mn

    row = lax.broadcasted_iota(I32, (rows2, PAGE_SIZE), 0)
    col = lax.broadcasted_iota(I32, (rows2, PAGE_SIZE), 1)
    tok = jnp.where(row >= ts, row - ts, row)
    dist = (past - p * PAGE_SIZE) + tok - col
    distf = dist.astype(F32)
    for h in range(n_h):
        s = stacked3(qm_ref[h], kc_ref[:, h, :], _dot_nt) - float(slopes[h]) * distf
        update(h, s, vc_ref[:, h, :])

    @pl.when(p == n_pages - 1)
    def _():
        lam = _lambda_value(lq1, lk1, lq2, lk2, lam_init)
        rown = lax.broadcasted_iota(I32, (rows2, rows2), 0)
        coln = lax.broadcasted_iota(I32, (rows2, rows2), 1)
        tokn = jnp.where(rown >= ts, rown - ts, rown)
        distn = tokn - coln
        valid = (distn >= 0) & (coln < ts)
        for h in range(n_h):
            kn = qkv_ref[:, pl.ds(d_attn + h * HEAD_DIM, HEAD_DIM)]
            vn = qkv_ref[:, pl.ds(2 * d_attn + h * HEAD_DIM, HEAD_DIM)]
            s = stacked3(qm_ref[h], kn, _dot_nt)
            s = jnp.where(valid, s - float(slopes[h]) * distn.astype(F32), NEG_INF)
            update(h, s, vn)
            o = acc_ref[h] / l_ref[h]
            o = o - lam * pltpu.roll(o, ts, 0)
            o_ref[:, pl.ds(h * HEAD_DIM, HEAD_DIM)] = _subln(o, gain_ref[...], lam_init)


def _paged_attention(proj_s, cache_k, cache_v, layer, page_table, lw, lam_init, slopes, ts, tpad, d_attn):
    n_seq, n_pages = page_table.shape
    n_h = d_attn // HEAD_DIM
    vec = lambda n: pl.BlockSpec((1, n), lambda s, p, pt: (0, 0))
    cache_spec = pl.BlockSpec((None, None, PAGE_SIZE, n_h, HEAD_DIM),
                              lambda s, p, pt: (layer, pt[s * n_pages + p], 0, 0, 0))
    return pl.pallas_call(
        functools.partial(_paged_attn_kernel, ts=ts, n_h=n_h, n_pages=n_pages, lam_init=lam_init,
                          slopes=tuple(float(x) for x in slopes)),
        grid_spec=pltpu.PrefetchScalarGridSpec(
            num_scalar_prefetch=1,
            grid=(n_seq, n_pages),
            in_specs=[vec(QK_HALF), vec(QK_HALF), vec(QK_HALF), vec(QK_HALF), vec(HEAD_DIM),
                      pl.BlockSpec((tpad, 3 * d_attn), lambda s, p, pt: (s, 0)),
                      cache_spec, cache_spec],
            out_specs=pl.BlockSpec((tpad, d_attn), lambda s, p, pt: (s, 0)),
            scratch_shapes=[pltpu.VMEM((n_h, 2 * ts, HEAD_DIM), F32),
                            pltpu.VMEM((n_h, 2 * ts, 1), F32),
                            pltpu.VMEM((n_h, 2 * ts, 1), F32),
                            pltpu.VMEM((n_h, 2 * ts, HEAD_DIM), F32)]),
        out_shape=jax.ShapeDtypeStruct((n_seq * tpad, d_attn), F32),
        compiler_params=_cparams(("parallel", "arbitrary"), 32),
    )(page_table.reshape(-1), lw['lambda_q1'].reshape(1, -1), lw['lambda_k1'].reshape(1, -1),
      lw['lambda_q2'].reshape(1, -1), lw['lambda_k2'].reshape(1, -1), lw['subln_gain'].reshape(1, -1),
      proj_s, cache_k, cache_v)


def _pool_kernel(u_ref, pre_ref, w_ref, sc_ref, o_ref, buf_ref, *, t, pos0, precise):
    g = pl.program_id(1)
    hist = pre_ref.shape[0]
    buf_ref[pl.ds(0, hist), :] = pre_ref[...]
    buf_ref[pl.ds(hist, t), :] = u_ref[...]
    pos = pos0 + 1 + lax.broadcasted_iota(I32, (t, 1), 0)
    for gi, w in enumerate(POOL_WINDOWS):
        @pl.when(g == gi)
        def _(w=w):
            acc = buf_ref[pl.ds(hist, t), :]
            for i in range(1, w):
                acc = acc + buf_ref[pl.ds(hist - i, t), :]
            cnt = jnp.minimum(pos, w).astype(F32)
            diff = acc / cnt - u_ref[...]
            if precise:
                mixed = _dot3(diff, w_ref[...])
            else:
                mixed = _dot(diff.astype(BF16), w_ref[...].astype(BF16))
            o_ref[...] = mixed * sc_ref[...]


def _pool_mix(proj, col0, prefix16, w_pool, scale, n_b, t, pos0, precise):
    n_g = len(POOL_WINDOWS)
    grp = w_pool.shape[-1]
    cb = col0 // grp
    return pl.pallas_call(
        functools.partial(_pool_kernel, t=t, pos0=pos0, precise=precise),
        grid=(n_b, n_g),
        in_specs=[pl.BlockSpec((t, grp), lambda b, g: (b, cb + g)),
                  pl.BlockSpec((None, 16, grp), lambda b, g: (b, 0, g)),
                  pl.BlockSpec((None, grp, grp), lambda b, g: (g, 0, 0)),
                  pl.BlockSpec((1, grp), lambda b, g: (0, g))],
        out_specs=pl.BlockSpec((t, grp), lambda b, g: (b, g)),
        out_shape=jax.ShapeDtypeStruct((n_b * t, n_g * grp), F32),
        scratch_shapes=[pltpu.VMEM((16 + t, grp), F32)],
        compiler_params=_cparams(("parallel", "parallel"), 32),
    )(proj, prefix16, w_pool, scale.reshape(1, -1))


def _dprep_kernel(x_ref, pre_ref, cw_ref, o_ref, buf_ref, *, t, n_h):
    j = pl.program_id(1)
    buf_ref[pl.ds(0, 8), :] = pre_ref[...]
    buf_ref[pl.ds(8, t), :] = x_ref[...]
    acc = buf_ref[pl.ds(8 - (CONV_W - 1), t), :] * cw_ref[pl.ds(0, 1), :]
    for i in range(1, CONV_W):
        acc = acc + buf_ref[pl.ds(8 - (CONV_W - 1) + i, t), :] * cw_ref[pl.ds(i, 1), :]
    s = _silu(acc)
    nrm = s * lax.rsqrt(jnp.sum(s * s, axis=-1, keepdims=True) + RMS_EPS)
    o_ref[...] = jnp.where(j < n_h, nrm * (HEAD_DIM ** -0.5), jnp.where(j < 2 * n_h, nrm, s))


def _delta_prep(proj, col0, prefix8, conv_w, n_b, t, d_delta):
    n_h = d_delta // HEAD_DIM
    cb = col0 // HEAD_DIM
    return pl.pallas_call(
        functools.partial(_dprep_kernel, t=t, n_h=n_h),
        grid=(n_b, 3 * n_h),
        in_specs=[pl.BlockSpec((t, HEAD_DIM), lambda b, j: (b, cb + j)),
                  pl.BlockSpec((None, 8, HEAD_DIM), lambda b, j: (b, 0, j)),
                  pl.BlockSpec((CONV_W, HEAD_DIM), lambda b, j: (0, j))],
        out_specs=pl.BlockSpec((t, HEAD_DIM), lambda b, j: (b, j)),
        out_shape=jax.ShapeDtypeStruct((n_b * t, 3 * d_delta), F32),
        scratch_shapes=[pltpu.VMEM((8 + t, HEAD_DIM), F32)],
        compiler_params=_cparams(("parallel", "parallel"), 32),
    )(proj, prefix8, conv_w)


def _gates_kernel(x_ref, alog_ref, dtb_ref, o_ref, *, n_h):
    x = x_ref[...]
    lane = lax.broadcasted_iota(I32, x.shape, 1)
    beta = _sigmoid(x)
    z = x + dtb_ref[...]
    softplus = jnp.maximum(z, 0.0) + jnp.log(1.0 + jnp.exp(-jnp.abs(z)))
    g = -jnp.exp(alog_ref[...]) * softplus
    o_ref[...] = jnp.where(lane < n_h, beta, g)


def _gates(proj, col0, a_log, dt_bias, rows, tm, n_h):
    cb = col0 // LANES
    pad = lambda v: jnp.zeros((1, LANES), F32).at[0, n_h:2 * n_h].set(v.astype(F32))
    return pl.pallas_call(
        functools.partial(_gates_kernel, n_h=n_h),
        grid=(rows // tm,),
        in_specs=[pl.BlockSpec((tm, LANES), lambda i: (i, cb)),
                  pl.BlockSpec((1, LANES), lambda i: (0, 0)),
                  pl.BlockSpec((1, LANES), lambda i: (0, 0))],
        out_specs=pl.BlockSpec((tm, LANES), lambda i: (i, 0)),
        out_shape=jax.ShapeDtypeStruct((rows, LANES), F32),
        compiler_params=_cparams(("parallel",), 32),
    )(proj, pad(a_log), pad(dt_bias))


def _unit_lower_inverse(lm, c):
    ii = lax.broadcasted_iota(I32, (c, c), 0)
    jj = lax.broadcasted_iota(I32, (c, c), 1)
    eye = (ii == jj).astype(F32)
    base = DELTA_BASE
    shift = int(math.log2(base))
    n0 = -jnp.where((ii >> shift) == (jj >> shift), lm, 0.0)
    inv = eye + n0
    npow = n0
    for _ in range(shift - 1):
        npow = _dot3(npow, npow)
        inv = inv + _dot3(inv, npow)
    size = base
    while size < c:
        s = int(math.log2(size))
        sib = ((ii >> (s + 1)) == (jj >> (s + 1))) & ((ii >> s) != (jj >> s))
        loff = jnp.where(sib, lm, 0.0)
        inv = inv - _dot3(inv, _dot3(loff, inv))
        size *= 2
    return inv


def _delta_kernel(q_ref, k_ref, v_ref, z_ref, gb_ref, gr_ref, nrm_ref, s0_ref, o_ref, sf_ref, *, t, c, heads,
                  precise):
    nc = t // c
    if precise:
        mm, mm_nt, mm_tn = _dot3, functools.partial(_dot3, dot=_dot_nt), functools.partial(_dot3, dot=_dot_tn)
    else:
        cast = lambda f: (lambda a, b: f(a.astype(BF16), b.astype(BF16)))
        mm, mm_nt, mm_tn = cast(_dot), cast(_dot_nt), cast(_dot_tn)
    ii = lax.broadcasted_iota(I32, (c, c), 0)
    jj = lax.broadcasted_iota(I32, (c, c), 1)
    incl = ii >= jj
    strict = ii > jj
    tri = incl.astype(BF16)
    upper = (ii <= jj).astype(BF16)
    sf_ref[...] = s0_ref[...]

    def chunk(ci, carry):
        r0 = pl.multiple_of(ci * c, SUBLANES)
        for g in range(heads):
            cols = pl.ds(g * HEAD_DIM, HEAD_DIM)
            q = q_ref[pl.ds(r0, c), cols]
            k = k_ref[pl.ds(r0, c), cols]
            v = v_ref[pl.ds(r0, c), cols]
            gb = gb_ref[g, pl.ds(r0, c), :]
            beta = gb[:, 0:1]
            gcol = gb[:, 1:2]
            grow = gr_ref[g, pl.ds(ci, 1), :]
            gch, gcl = _split_bf16(jnp.broadcast_to(gcol, (c, c)))
            cum_i = _dot(tri, gch) + _dot(tri, gcl)
            grh, grl = _split_bf16(jnp.broadcast_to(grow, (c, c)))
            cum_j = _dot(grh, upper) + _dot(grl, upper)
            decay = jnp.exp(jnp.where(incl, cum_i - cum_j, NEG_INF))
            cum = cum_i[:, 0:1]
            cum_last = cum_i[c - 1:c, 0:1]
            ecum = jnp.exp(cum)
            kk = _dot3(k, k, _dot_nt)
            lower = jnp.where(strict, beta * kk * decay, 0.0)
            inv = _unit_lower_inverse(lower, c)
            rhs = jnp.concatenate([v * beta, k * (beta * ecum)], axis=1)
            sol = _dot3(inv, rhs)
            new_v = sol[:, :HEAD_DIM]
            k_cum = sol[:, HEAD_DIM:]
            qk = mm_nt(q, k) * decay
            q_dec = q * ecum
            k_dec = k * jnp.exp(cum_last - cum)
            g_tot = jnp.exp(cum_last)
            state = sf_ref[g]
            w = new_v - mm(k_cum, state)
            out = mm(q_dec, state) + mm(qk, w)
            sf_ref[g] = state * g_tot + mm_tn(k_dec, w)
            out = out * lax.rsqrt(jnp.mean(out * out, axis=-1, keepdims=True) + RMS_EPS) * nrm_ref[...]
            o_ref[pl.ds(r0, c), cols] = out * _silu(z_ref[pl.ds(r0, c), cols])
        return carry

    lax.fori_loop(0, nc, chunk, 0)


def _delta_rule(dq, proj, zcol0, gates, delta_norm, s0, n_b, t, c, d_delta, precise):
    n_h = d_delta // HEAD_DIM
    hg = DELTA_HEADS_PER_STEP
    w = hg * HEAD_DIM
    nc = t // c
    gb = gates[:, :2 * n_h].reshape(n_b, t, 2, n_h).transpose(0, 3, 1, 2)
    gr = gates[:, n_h:2 * n_h].reshape(n_b, nc, c, n_h).transpose(0, 3, 1, 2)
    kb, vb, zb = d_delta // w, 2 * d_delta // w, zcol0 // w
    return pl.pallas_call(
        functools.partial(_delta_kernel, t=t, c=c, heads=hg, precise=precise),
        grid=(n_b, n_h // hg),
        in_specs=[pl.BlockSpec((t, w), lambda b, h: (b, h)),
                  pl.BlockSpec((t, w), lambda b, h: (b, kb + h)),
                  pl.BlockSpec((t, w), lambda b, h: (b, vb + h)),
                  pl.BlockSpec((t, w), lambda b, h: (b, zb + h)),
                  pl.BlockSpec((None, hg, t, 2), lambda b, h: (b, h, 0, 0)),
                  pl.BlockSpec((None, hg, nc, c), lambda b, h: (b, h, 0, 0)),
                  pl.BlockSpec((1, HEAD_DIM), lambda b, h: (0, 0)),
                  pl.BlockSpec((None, hg, HEAD_DIM, HEAD_DIM), lambda b, h: (b, h, 0, 0))],
        out_specs=[pl.BlockSpec((t, w), lambda b, h: (b, h)),
                   pl.BlockSpec((None, hg, HEAD_DIM, HEAD_DIM), lambda b, h: (b, h, 0, 0))],
        out_shape=[jax.ShapeDtypeStruct((n_b * t, d_delta), F32),
                   jax.ShapeDtypeStruct((n_b, n_h, HEAD_DIM, HEAD_DIM), F32)],
        compiler_params=_cparams(("parallel", "parallel"), 56),
    )(dq, dq, dq, proj, gb, gr, delta_norm.reshape(1, -1), s0)


def _router_kernel(h_ref, g_ref, wr_ref, xn_ref, meta_ref, cnt_ref, *, tm):
    @pl.when(pl.program_id(0) == 0)
    def _():
        cnt_ref[...] = jnp.zeros(cnt_ref.shape, F32)

    x = h_ref[...]
    xn = x * lax.rsqrt(jnp.mean(x * x, axis=-1, keepdims=True) + RMS_EPS) * g_ref[...]
    xn_ref[...] = xn
    logits = _dot3(xn, wr_ref[...])
    lane = lax.broadcasted_iota(I32, (tm, LANES), 1)
    big = jnp.int32(LANES)

    def first_max(vals, valid):
        vmax = jnp.max(jnp.where(valid, vals, -jnp.inf), axis=-1, keepdims=True)
        idx = jnp.min(jnp.where(valid & (vals == vmax), lane, big), axis=-1, keepdims=True)
        return vmax, idx

    gvalid = lane < N_GROUPS
    gmax, gstar = first_max(logits, gvalid)
    p_sel = 1.0 / jnp.sum(jnp.where(gvalid, jnp.exp(logits - gmax), 0.0), axis=-1, keepdims=True)
    e0 = N_GROUPS + gstar * EXPERTS_PER_GROUP
    evalid = (lane >= e0) & (lane < e0 + EXPERTS_PER_GROUP)
    emax, _ = first_max(logits, evalid)
    pe = jnp.where(evalid, jnp.exp(logits - emax), 0.0)
    pe = pe / jnp.sum(pe, axis=-1, keepdims=True)
    v1, i1 = first_max(pe, evalid)
    v2, i2 = first_max(pe, evalid & (lane != i1))
    den = v1 + v2
    g1 = p_sel * v1 / den
    g2 = p_sel * v2 / den
    ex1 = i1 - N_GROUPS
    ex2 = i2 - N_GROUPS
    oh1 = lane == ex1
    oh2 = lane == ex2
    oh = jnp.where(oh1 | oh2, 1.0, 0.0)
    ri = lax.broadcasted_iota(I32, (tm, tm), 0)
    rj = lax.broadcasted_iota(I32, (tm, tm), 1)
    before = _dot((ri > rj).astype(BF16), oh.astype(BF16)) + cnt_ref[...]
    r1 = jnp.sum(jnp.where(oh1, before, 0.0), axis=-1, keepdims=True)
    r2 = jnp.sum(jnp.where(oh2, before, 0.0), axis=-1, keepdims=True)
    cnt_ref[...] = cnt_ref[...] + jnp.sum(oh, axis=0, keepdims=True)
    meta = jnp.where(lane == 0, ex1.astype(F32),
           jnp.where(lane == 1, ex2.astype(F32),
           jnp.where(lane == 2, g1,
           jnp.where(lane == 3, g2,
           jnp.where(lane == 4, r1,
           jnp.where(lane == 5, r2, 0.0))))))
    meta_ref[...] = meta


def _router(h, gain, w_rg, w_re, tm):
    m, d = h.shape
    wr = jnp.zeros((d, LANES), F32)
    wr = wr.at[:, :N_GROUPS].set(w_rg)
    wr = wr.at[:, N_GROUPS:N_GROUPS + N_EXPERTS].set(w_re.transpose(1, 0, 2).reshape(d, N_EXPERTS))
    return pl.pallas_call(
        functools.partial(_router_kernel, tm=tm),
        grid=(m // tm,),
        in_specs=[pl.BlockSpec((tm, d), lambda i: (i, 0)),
                  pl.BlockSpec((1, d), lambda i: (0, 0)),
                  pl.BlockSpec((d, LANES), lambda i: (0, 0))],
        out_specs=[pl.BlockSpec((tm, d), lambda i: (i, 0)),
                   pl.BlockSpec((tm, LANES), lambda i: (i, 0)),
                   pl.BlockSpec((1, LANES), lambda i: (0, 0))],
        out_shape=[jax.ShapeDtypeStruct((m, d), F32),
                   jax.ShapeDtypeStruct((m, LANES), F32),
                   jax.ShapeDtypeStruct((1, LANES), F32)],
        compiler_params=_cparams(("arbitrary",), 56),
    )(h, gain.reshape(1, d), wr)


def _row_copy(src_hbm, dst, sem, src_row, dst_row):
    return pltpu.make_async_copy(src_hbm.at[pl.ds(src_row, 1), :], dst.at[pl.ds(dst_row, 1), :], sem)


def _gather_kernel(tok_ref, used_ref, x_hbm, o_ref, buf_ref, sem, *, rows):
    base = pl.program_id(0) * rows

    @pl.when(base < used_ref[0])
    def _():
        def issue(r, c):
            _row_copy(x_hbm, buf_ref, sem, tok_ref[base + r], r).start()
            return c
        lax.fori_loop(0, rows, issue, 0)
        pltpu.make_async_copy(x_hbm.at[pl.ds(0, rows), :], buf_ref, sem).wait()
        o_ref[...] = buf_ref[...].astype(o_ref.dtype)


def _dispatch(xn, tok_of_row, used, n_rows, rows, out_dtype):
    d = xn.shape[1]
    return pl.pallas_call(
        functools.partial(_gather_kernel, rows=rows),
        grid_spec=pltpu.PrefetchScalarGridSpec(
            num_scalar_prefetch=2,
            grid=(n_rows // rows,),
            in_specs=[pl.BlockSpec(memory_space=pl.ANY)],
            out_specs=pl.BlockSpec((rows, d), lambda i, tok, used: (i, 0)),
            scratch_shapes=[pltpu.VMEM((rows, d), F32), pltpu.SemaphoreType.DMA(())]),
        out_shape=jax.ShapeDtypeStruct((n_rows, d), out_dtype),
        compiler_params=_cparams(("arbitrary",), 32),
    )(tok_of_row, used, xn)


def _for_valid_rows(nrows, total, sub, fn):
    need = (nrows + sub - 1) // sub
    for q in range(1, total // sub + 1):
        @pl.when(need == q)
        def _(q=q):
            fn(q * sub)


def _expert_up_kernel(be_ref, br_ref, x_ref, wg_ref, wu_ref, h_ref, *, precise):
    w = pl.program_id(1)

    def run(rows):
        x = x_ref[pl.ds(0, rows), :]
        if precise:
            hg = _dot3(x, wg_ref[...])
            hu = _dot3(x, wu_ref[...])
        else:
            hg = _dot(x, wg_ref[...].astype(BF16))
            hu = _dot(x, wu_ref[...].astype(BF16))
        h_ref[pl.ds(0, rows), :] = (_silu(hg) * hu).astype(h_ref.dtype)

    _for_valid_rows(br_ref[w], x_ref.shape[0], MOE_SUB, run)


def _expert_down_kernel(be_ref, br_ref, h_ref, wd_ref, y_ref, *, precise):
    w = pl.program_id(1)

    def run(rows):
        hid = h_ref[pl.ds(0, rows), :]
        if precise:
            y_ref[pl.ds(0, rows), :] = _dot3(hid, wd_ref[...])
        else:
            y_ref[pl.ds(0, rows), :] = _dot(hid, wd_ref[...].astype(BF16))

    _for_valid_rows(br_ref[w], h_ref.shape[0], MOE_SUB, run)


def _experts(xs, blk_e, blk_rows, w_gate, w_up, w_down, r, precise):
    n_rows, d = xs.shape
    nb = n_rows // r
    d_e = w_gate.shape[-1]
    tc, tn = MOE_HID_TILE, MOE_OUT_TILE
    hid = pl.pallas_call(
        functools.partial(_expert_up_kernel, precise=precise),
        grid_spec=pltpu.PrefetchScalarGridSpec(
            num_scalar_prefetch=2,
            grid=(d_e // tc, nb),
            in_specs=[pl.BlockSpec((r, d), lambda c, w, be, br: (w, 0)),
                      pl.BlockSpec((None, d, tc), lambda c, w, be, br: (be[w], 0, c)),
                      pl.BlockSpec((None, d, tc), lambda c, w, be, br: (be[w], 0, c))],
            out_specs=pl.BlockSpec((r, tc), lambda c, w, be, br: (w, c))),
        out_shape=jax.ShapeDtypeStruct((n_rows, d_e), F32 if precise else BF16),
        compiler_params=_cparams(("arbitrary", "arbitrary"), 48),
    )(blk_e, blk_rows, xs, w_gate, w_up)
    return pl.pallas_call(
        functools.partial(_expert_down_kernel, precise=precise),
        grid_spec=pltpu.PrefetchScalarGridSpec(
            num_scalar_prefetch=2,
            grid=(d // tn, nb),
            in_specs=[pl.BlockSpec((r, d_e), lambda n, w, be, br: (w, 0)),
                      pl.BlockSpec((None, d_e, tn), lambda n, w, be, br: (be[w], 0, n))],
            out_specs=pl.BlockSpec((r, tn), lambda n, w, be, br: (w, n))),
        out_shape=jax.ShapeDtypeStruct((n_rows, d), F32),
        compiler_params=_cparams(("arbitrary", "arbitrary"), 48),
    )(blk_e, blk_rows, hid, w_down)


def _combine_kernel(dest_ref, y_hbm, h_ref, meta_ref, gf_ref, o_ref, buf_ref, sem, *, rows, final):
    base = pl.program_id(0) * rows

    def issue(r, c):
        t2 = 2 * (base + r)
        _row_copy(y_hbm, buf_ref.at[0], sem, dest_ref[t2], r).start()
        _row_copy(y_hbm, buf_ref.at[1], sem, dest_ref[t2 + 1], r).start()
        return c
    lax.fori_loop(0, rows, issue, 0)
    pltpu.make_async_copy(y_hbm.at[pl.ds(0, rows), :], buf_ref.at[0], sem).wait()
    pltpu.make_async_copy(y_hbm.at[pl.ds(0, rows), :], buf_ref.at[1], sem).wait()
    meta = meta_ref[...]
    out = h_ref[...] + (buf_ref[0] * meta[:, 2:3] + buf_ref[1] * meta[:, 3:4])
    if final:
        out = out * lax.rsqrt(jnp.mean(out * out, axis=-1, keepdims=True) + RMS_EPS) * gf_ref[...]
    o_ref[...] = out


def _combine(y, dest, h, meta, norm_final, rows, final):
    m, d = h.shape
    return pl.pallas_call(
        functools.partial(_combine_kernel, rows=rows, final=final),
        grid_spec=pltpu.PrefetchScalarGridSpec(
            num_scalar_prefetch=1,
            grid=(m // rows,),
            in_specs=[pl.BlockSpec(memory_space=pl.ANY),
                      pl.BlockSpec((rows, d), lambda i, dest: (i, 0)),
                      pl.BlockSpec((rows, LANES), lambda i, dest: (i, 0)),
                      pl.BlockSpec((1, d), lambda i, dest: (0, 0))],
            out_specs=pl.BlockSpec((rows, d), lambda i, dest: (i, 0)),
            scratch_shapes=[pltpu.VMEM((2, rows, d), F32), pltpu.SemaphoreType.DMA(())]),
        out_shape=jax.ShapeDtypeStruct((m, d), F32),
        compiler_params=_cparams(("arbitrary",), 48),
    )(dest, y, h, meta, norm_final.reshape(1, d))


def _moe(h, norm_ffn, w_rg, w_re, w_gate, w_up, w_down, norm_final, *, final, precise, r, token_rows, gather_rows):
    m, d = h.shape
    xn, meta, counts = _router(h, norm_ffn, w_rg, w_re, token_rows)
    experts = meta[:, 0:2].astype(I32)
    rank = meta[:, 4:6].astype(I32)
    cnt = counts[0, :N_EXPERTS].astype(I32)
    padded = (cnt + r - 1) // r * r
    pad_end = jnp.cumsum(padded)
    pad_start = pad_end - padded
    dest = pad_start[experts] + rank
    nb = -(-(m * TOP_K) // r) + N_EXPERTS
    blk_start = jnp.arange(nb, dtype=I32) * r
    used = pad_end[-1]
    active = blk_start < used
    blk_e = jnp.minimum(jnp.searchsorted(pad_end, blk_start, side='right'), N_EXPERTS - 1).astype(I32)
    blk_rows = jnp.where(active, jnp.clip(cnt[blk_e] - (blk_start - pad_start[blk_e]), 0, r), 0).astype(I32)
    last_e = blk_e[jnp.maximum(used // r - 1, 0)]
    blk_e = jnp.where(active, blk_e, last_e)
    tok_of_row = jnp.zeros((nb * r,), I32).at[dest.reshape(-1)].set(
        jnp.repeat(jnp.arange(m, dtype=I32), TOP_K))
    xs = _dispatch(xn, tok_of_row, used.reshape(1).astype(I32), nb * r, gather_rows, F32 if precise else BF16)
    y = _experts(xs, blk_e, blk_rows, w_gate, w_up, w_down, r, precise)
    return _combine(y, dest.reshape(-1).astype(I32), h, meta, norm_final, token_rows, final)


def kernel(x_prompt, x_sample, cache_k, cache_v, state_pool, state_conv, state_delta, page_table, meta, norm_mix, w_in, lambda_q1, lambda_k1, lambda_q2, lambda_k2, subln_gain, w_pool, pool_scale, conv_w, a_log, dt_bias, delta_norm, w_out, norm_ffn, w_router_group, w_router_expert, w_gate, w_up, w_down, norm_final):
    depth = w_in.shape[0]
    n_b, seq, d_model = x_prompt.shape
    n_s, ts, _ = x_sample.shape
    tp = N_META + seq
    tpad = SUBLANES
    n_mix = d_model // HEAD_DIM
    d_attn = (3 * n_mix) // 8 * HEAD_DIM
    d_delta = d_attn
    d_pool = d_model - d_attn - d_delta
    n_dh = d_delta // HEAD_DIM
    d_in = w_in.shape[-1]
    col_u = 3 * d_attn
    col_c = col_u + d_pool
    col_z = col_c + 3 * d_delta
    col_ba = col_z + d_delta
    d_in_pad = -(-d_in // PROJ_COL_TILE) * PROJ_COL_TILE
    mp = n_b * tp
    past = page_table.shape[1] * PAGE_SIZE
    slopes = _alibi_slopes(d_attn // HEAD_DIM)

    hp = jnp.concatenate([jnp.broadcast_to(meta[None], (n_b, N_META, d_model)), x_prompt], axis=1)
    xp = hp.reshape(mp, d_model)
    xs = x_sample.reshape(n_s * ts, d_model)

    def pad_rows(a):
        a = a.reshape(n_s, ts, a.shape[-1])
        return jnp.pad(a, ((0, 0), (0, tpad - ts), (0, 0))).reshape(n_s * tpad, a.shape[-1])

    def real_rows(a):
        return a.reshape(n_s, tpad, -1)[:, :ts].reshape(n_s * ts, -1)

    def mixers(proj, attend, pool_prefix, conv_prefix, s0, n_seq, t, chunk, pos0, live, precise):
        oa = attend(proj)
        ob = _pool_mix(proj, col_u, pool_prefix, w_pool[l], pool_scale[l], n_seq, t, pos0, precise)
        dq = _delta_prep(proj, col_c, conv_prefix, conv_w[l], n_seq, t, d_delta)
        gates = _gates(proj, col_ba, a_log[l], dt_bias[l], n_seq * t, t, n_dh)
        if live is not None:
            gates = jnp.where(live, gates, 0.0)
        oc, sf = _delta_rule(dq, proj, col_z, gates, delta_norm[l], s0, n_seq, t, chunk, d_delta, precise)
        return jnp.concatenate([oa, ob, oc], axis=1), sf

    moe_p = dict(precise=False, r=MOE_ROWS, token_rows=PROMPT_TOKEN_ROWS, gather_rows=GATHER_ROWS)
    moe_s = dict(precise=True, r=MOE_SUB, token_rows=n_s * ts, gather_rows=MOE_SUB)

    outs = {k: [] for k in ('kp', 'vp', 'pp', 'cp', 'dp', 'ks', 'vs', 'ps', 'cs', 'ds')}
    for l in range(depth):
        lw = {'lambda_q1': lambda_q1[l], 'lambda_k1': lambda_k1[l], 'lambda_q2': lambda_q2[l],
              'lambda_k2': lambda_k2[l], 'subln_gain': subln_gain[l]}
        lam_init = 0.8 - 0.6 * math.exp(-0.3 * l)
        final = l == depth - 1
        moe_w = (norm_ffn[l], w_router_group[l], w_router_expert[l], w_gate[l], w_up[l], w_down[l], norm_final)

        w_in_b = jnp.pad(w_in[l].astype(BF16), ((0, 0), (0, d_in_pad - d_in)))
        proj = _dense(xp, w_in_b, gain=norm_mix[l], tm=PROMPT_ROW_TILE)
        act, sf_p = mixers(
            proj, lambda p: _prompt_attention(p, lw, lam_init, jnp.asarray(slopes), n_b, tp, d_attn),
            jnp.zeros((n_b, 16, d_pool), F32), jnp.zeros((n_b, 8, 3 * d_delta), F32),
            jnp.zeros((n_b, n_dh, HEAD_DIM, HEAD_DIM), F32), n_b, tp, DELTA_CHUNK, 0, None, False)
        h = _dense(act, w_out[l].astype(BF16), res=xp, tm=PROMPT_ROW_TILE)
        xp = _moe(h, *moe_w, final=final, **moe_p)

        proj_s32 = _dense(xs, w_in[l], gain=norm_mix[l], tm=n_s * ts, precise=True)
        proj_s3 = proj_s32.reshape(n_s, ts, d_in_pad)
        proj_s = pad_rows(proj_s32)
        pre_s = jnp.concatenate([jnp.zeros((n_s, 16 - POOL_BUF, d_pool), F32), state_pool[l]], axis=1)
        cpre_s = jnp.concatenate([jnp.zeros((n_s, 8 - (CONV_W - 1), 3 * d_delta), F32), state_conv[l]], axis=1)
        live = (jnp.arange(n_s * tpad) % tpad < ts)[:, None]
        act_s, sf_s = mixers(
            proj_s, lambda p: _paged_attention(p, cache_k, cache_v, l, page_table, lw, lam_init, slopes, ts, tpad,
                                               d_attn),
            pre_s, cpre_s, state_delta[l], n_s, tpad, tpad, past, live, True)
        h_s = _dense(real_rows(act_s), w_out[l], res=xs, tm=n_s * ts, precise=True)
        xs = _moe(h_s, *moe_w, final=final, **moe_s)

        pp3 = proj.reshape(n_b, tp, d_in_pad)
        n_ah = d_attn // HEAD_DIM
        outs['kp'].append(pp3[:, :, d_attn:2 * d_attn].reshape(n_b, tp, n_ah, HEAD_DIM))
        outs['vp'].append(pp3[:, :, 2 * d_attn:3 * d_attn].reshape(n_b, tp, n_ah, HEAD_DIM))
        outs['pp'].append(pp3[:, tp - POOL_BUF:, col_u:col_u + d_pool])
        outs['cp'].append(pp3[:, tp - (CONV_W - 1):, col_c:col_c + 3 * d_delta])
        outs['dp'].append(sf_p)
        outs['ks'].append(proj_s3[:, :, d_attn:2 * d_attn].reshape(n_s, ts, n_ah, HEAD_DIM))
        outs['vs'].append(proj_s3[:, :, 2 * d_attn:3 * d_attn].reshape(n_s, ts, n_ah, HEAD_DIM))
        u_s = proj_s3[:, :, col_u:col_u + d_pool]
        outs['ps'].append(jnp.concatenate([state_pool[l], u_s], axis=1)[:, -POOL_BUF:])
        c_s = proj_s3[:, :, col_c:col_c + 3 * d_delta]
        outs['cs'].append(jnp.concatenate([state_conv[l], c_s], axis=1)[:, -(CONV_W - 1):])
        outs['ds'].append(sf_s)

    y_prompt = xp.reshape(n_b, tp, d_model)[:, N_META:]
    y_sample = xs.reshape(n_s, ts, d_model)
    st = lambda k: jnp.stack(outs[k])
    return (y_prompt, y_sample, st('kp'), st('vp'), st('pp'), st('cp'), st('dp'),
            st('ks'), st('vs'), st('ps'), st('cs'), st('ds'))
```

```python
import functools
import math

import numpy as np
import jax
import jax.numpy as jnp
from jax import lax
from jax.experimental import pallas as pl
from jax.experimental.pallas import tpu as pltpu

F32 = jnp.float32
BF16 = jnp.bfloat16
I32 = jnp.int32

HEAD_DIM = 128
QK_HALF = HEAD_DIM // 2
POOL_WINDOWS = (2, 4, 8, 16)
POOL_BUF = 15
CONV_W = 4
N_META = 16
N_GROUPS = 4
EXPERTS_PER_GROUP = 8
N_EXPERTS = N_GROUPS * EXPERTS_PER_GROUP
TOP_K = 2
PAGE_SIZE = 128
RMS_EPS = 1e-6
NEG_INF = -1e30

LANES = 128
SUBLANES = 8
MIB = 1024 * 1024

PROMPT_ROW_TILE = 688
PROMPT_TOKEN_ROWS = 344
PROJ_COL_TILE = 512
ATTN_BLOCK = 256
DELTA_CHUNK = 48
DELTA_BASE = 8
PAGES_PER_STEP = 4
MOE_ROWS = 512
MOE_SUB = 128
MOE_HID_TILE = 256
MOE_OUT_TILE = 1024
GATHER_ROWS = 256


def _cparams(semantics, vmem_mib):
    return pltpu.CompilerParams(dimension_semantics=semantics, vmem_limit_bytes=vmem_mib * MIB)


def _alibi_slopes(n_heads):
    def pow2_slopes(m):
        start = 2.0 ** (-8.0 / m)
        return [start ** (i + 1) for i in range(m)]
    closest = 2 ** int(math.floor(math.log2(n_heads)))
    slopes = pow2_slopes(closest) + pow2_slopes(2 * closest)[0::2][: n_heads - closest]
    return np.array(slopes, dtype=np.float32)


def _sigmoid(x):
    return 1.0 / (1.0 + jnp.exp(-x))


def _silu(x):
    return x * _sigmoid(x)


def _split_bf16(x):
    hi = x.astype(BF16)
    lo = (x - hi.astype(F32)).astype(BF16)
    return hi, lo


def _stack_hilo(x):
    hi = x.astype(BF16).astype(F32)
    return jnp.concatenate([hi, x - hi], axis=0).astype(BF16)


def _dot(a, b):
    return jnp.dot(a, b, preferred_element_type=F32)


def _dot_nt(a, b):
    return lax.dot_general(a, b, (((1,), (1,)), ((), ())), preferred_element_type=F32)


def _dot_tn(a, b):
    return lax.dot_general(a, b, (((0,), (0,)), ((), ())), preferred_element_type=F32)


def _dot3(a, b, dot=_dot):
    ah, al = _split_bf16(a)
    bh, bl = _split_bf16(b)
    return dot(ah, bh) + (dot(ah, bl) + dot(al, bh))


def _proj_kernel(*refs, normed, residual, precise, w_transposed):
    mm = _dot_nt if w_transposed else _dot
    it = iter(refs)
    x_ref = next(it)
    g_ref = next(it) if normed else None
    w_ref = next(it)
    r_ref = next(it) if residual else None
    o_ref = next(it)
    xh_ref = next(it)
    xl_ref = next(it) if precise else None

    @pl.when(pl.program_id(1) == 0)
    def _():
        x = x_ref[...]
        if normed:
            ms = jnp.mean(x * x, axis=-1, keepdims=True)
            x = x * lax.rsqrt(ms + RMS_EPS) * g_ref[...]
        if precise:
            xh_ref[...], xl_ref[...] = _split_bf16(x)
        else:
            xh_ref[...] = x.astype(BF16)

    if precise:
        wh, wl = _split_bf16(w_ref[...])
        acc = mm(xh_ref[...], wh) + (mm(xh_ref[...], wl) + mm(xl_ref[...], wh))
    else:
        acc = mm(xh_ref[...], w_ref[...])
    if residual:
        acc = r_ref[...] + acc
    o_ref[...] = acc


def _dense(x, w, gain=None, res=None, *, tm, tn=PROJ_COL_TILE, precise=False, layer=None, w_transposed=False):
    m, d = x.shape
    n = w.shape[-2] if w_transposed else w.shape[-1]
    n_out = pl.cdiv(n, tn) * tn
    assert m % tm == 0
    in_specs = [pl.BlockSpec((tm, d), lambda i, j: (i, 0))]
    args = [x]
    if gain is not None:
        in_specs.append(pl.BlockSpec((1, d), lambda i, j: (0, 0)))
        args.append(gain.reshape(1, d))
    if layer is None:
        in_specs.append(pl.BlockSpec((d, tn), lambda i, j: (0, j)))
    elif w_transposed:
        in_specs.append(pl.BlockSpec((None, tn, d), lambda i, j: (layer, j, 0)))
    else:
        in_specs.append(pl.BlockSpec((None, d, tn), lambda i, j: (layer, 0, j)))
    args.append(w)
    if res is not None:
        in_specs.append(pl.BlockSpec((tm, tn), lambda i, j: (i, j)))
        args.append(res)
    return pl.pallas_call(
        functools.partial(_proj_kernel, normed=gain is not None, residual=res is not None, precise=precise,
                          w_transposed=w_transposed),
        grid=(m // tm, n_out // tn),
        in_specs=in_specs,
        out_specs=pl.BlockSpec((tm, tn), lambda i, j: (i, j)),
        out_shape=jax.ShapeDtypeStruct((m, n_out), F32),
        scratch_shapes=[pltpu.VMEM((tm, d), BF16)] * (2 if precise else 1),
        name="dense_precise" if precise else "dense",
        compiler_params=_cparams(("parallel", "arbitrary"), 56),
    )(*args)


def _lambda_value(lq1, lk1, lq2, lk2, lam_init):
    a = jnp.sum(lq1[...] * lk1[...], axis=-1, keepdims=True)
    b = jnp.sum(lq2[...] * lk2[...], axis=-1, keepdims=True)
    return jnp.exp(a) - jnp.exp(b) + lam_init


def _subln(o, gain, lam_init):
    o = o * lax.rsqrt(jnp.mean(o * o, axis=-1, keepdims=True) + RMS_EPS) * gain
    return o * (1.0 - lam_init)


def _prompt_attn_kernel(slopes_ref, lq1, lk1, lq2, lk2, gain_ref, q_ref, k_ref, v_ref, o_ref, ko_ref, vo_ref,
                        qp_ref, kp_ref, vp_ref, op_ref, *, t, tb, lam_init):
    h = pl.program_id(1)
    ko_ref[...] = k_ref[...]
    vo_ref[...] = v_ref[...]
    slope = slopes_ref[h]
    lam = _lambda_value(lq1, lk1, lq2, lk2, lam_init)
    tp = qp_ref.shape[0]
    nb = tp // tb
    for src, dst in ((q_ref, qp_ref), (k_ref, kp_ref), (v_ref, vp_ref)):
        dst[pl.ds(0, t), :] = src[...]
        dst[pl.ds(t, tp - t), :] = jnp.zeros((tp - t, HEAD_DIM), F32)
    dloc = lax.broadcasted_iota(I32, (tb, tb), 0) - lax.broadcasted_iota(I32, (tb, tb), 1)

    def qblock(qi, carry):
        r0 = pl.multiple_of(qi * tb, tb)
        q = qp_ref[pl.ds(r0, tb), :] * (QK_HALF ** -0.5)
        q1 = q[:, :QK_HALF].astype(BF16)
        q2 = q[:, QK_HALF:].astype(BF16)

        def kvblock(kj, c):
            m1, l1, a1, m2, l2, a2 = c
            c0 = pl.multiple_of(kj * tb, tb)
            kk = kp_ref[pl.ds(c0, tb), :]
            vv = vp_ref[pl.ds(c0, tb), :].astype(BF16)
            dist = dloc + (r0 - c0)
            causal = dist >= 0
            bias = slope * dist.astype(F32)

            def upd(qh, kh, m, l, a):
                s = _dot_nt(qh, kh)
                s = jnp.where(causal, s - bias, NEG_INF)
                mn = jnp.maximum(m, jnp.max(s, axis=-1, keepdims=True))
                p = jnp.exp(s - mn)
                al = jnp.exp(m - mn)
                l = al * l + jnp.sum(p, axis=-1, keepdims=True)
                a = al * a + _dot(p.astype(BF16), vv)
                return mn, l, a

            m1, l1, a1 = upd(q1, kk[:, :QK_HALF].astype(BF16), m1, l1, a1)
            m2, l2, a2 = upd(q2, kk[:, QK_HALF:].astype(BF16), m2, l2, a2)
            return m1, l1, a1, m2, l2, a2

        m0 = jnp.full((tb, 1), NEG_INF, F32)
        z1 = jnp.zeros((tb, 1), F32)
        za = jnp.zeros((tb, HEAD_DIM), F32)
        m1, l1, a1, m2, l2, a2 = lax.fori_loop(0, qi + 1, kvblock, (m0, z1, za, m0, z1, za))
        o = a1 / l1 - lam * (a2 / l2)
        op_ref[pl.ds(r0, tb), :] = _subln(o, gain_ref[...], lam_init)
        return carry

    lax.fori_loop(0, nb, qblock, 0)
    o_ref[...] = op_ref[pl.ds(0, t), :]


def _prompt_attention(proj, lw, lam_init, slopes, n_b, t, d_attn):
    n_h = d_attn // HEAD_DIM
    tb = ATTN_BLOCK
    tp = -(-t // tb) * tb
    vec = lambda n: pl.BlockSpec((1, n), lambda b, h: (0, 0))
    blk = lambda off: pl.BlockSpec((t, HEAD_DIM), lambda b, h, off=off: (b, off + h))
    return pl.pallas_call(
        functools.partial(_prompt_attn_kernel, t=t, tb=tb, lam_init=lam_init),
        grid=(n_b, n_h),
        in_specs=[pl.BlockSpec(memory_space=pltpu.SMEM),
                  vec(QK_HALF), vec(QK_HALF), vec(QK_HALF), vec(QK_HALF), vec(HEAD_DIM),
                  blk(0), blk(n_h), blk(2 * n_h)],
        out_specs=[pl.BlockSpec((t, HEAD_DIM), lambda b, h: (b, h)),
                   pl.BlockSpec((None, None, t, HEAD_DIM), lambda b, h: (b, h, 0, 0)),
                   pl.BlockSpec((None, None, t, HEAD_DIM), lambda b, h: (b, h, 0, 0))],
        out_shape=[jax.ShapeDtypeStruct((n_b * t, d_attn), F32),
                   jax.ShapeDtypeStruct((n_b, n_h, t, HEAD_DIM), F32),
                   jax.ShapeDtypeStruct((n_b, n_h, t, HEAD_DIM), F32)],
        scratch_shapes=[pltpu.VMEM((tp, HEAD_DIM), F32)] * 4,
        name="prompt_attention",
        compiler_params=_cparams(("parallel", "parallel"), 40),
    )(slopes, lw['lambda_q1'].reshape(1, -1), lw['lambda_k1'].reshape(1, -1),
      lw['lambda_q2'].reshape(1, -1), lw['lambda_k2'].reshape(1, -1), lw['subln_gain'].reshape(1, -1),
      proj, proj, proj)


def _paged_attn_kernel(pt_ref, lq1, lk1, lq2, lk2, gain_ref, qkv_ref, *rest, ts, n_h, n_pages, pps, lam_init,
                       slopes):
    k_refs, v_refs = rest[:pps], rest[pps:2 * pps]
    o_ref, qm_ref, m_ref, l_ref, acc_ref = rest[2 * pps:]
    p = pl.program_id(1)
    d_attn = n_h * HEAD_DIM
    past = n_pages * PAGE_SIZE
    width = pps * PAGE_SIZE
    rows2 = 2 * ts

    @pl.when(p == 0)
    def _():
        lane = lax.broadcasted_iota(I32, (rows2, HEAD_DIM), 1)
        srow = lax.broadcasted_iota(I32, (rows2, HEAD_DIM), 0)
        for h in range(n_h):
            q = qkv_ref[:, pl.ds(h * HEAD_DIM, HEAD_DIM)] * (QK_HALF ** -0.5)
            qa = jnp.where(lane < QK_HALF, q, 0.0)
            qb = jnp.where(lane >= QK_HALF, pltpu.roll(q, ts, 0), 0.0)
            qm_ref[h] = jnp.where(srow < ts, qa, qb)
        m_ref[...] = jnp.full(m_ref.shape, NEG_INF, F32)
        l_ref[...] = jnp.zeros(l_ref.shape, F32)
        acc_ref[...] = jnp.zeros(acc_ref.shape, F32)

    def stacked3(lhs, rhs, dot):
        rh, rl = _split_bf16(rhs)
        both = dot(_stack_hilo(lhs), rh)
        return both[:rows2] + (both[rows2:] + dot(lhs.astype(BF16), rl))

    def softmax_step(h, s, v, m_old, l_old, acc_old):
        mn = jnp.maximum(m_old, jnp.max(s, axis=-1, keepdims=True))
        pr = jnp.exp(s - mn)
        al = jnp.exp(m_old - mn)
        return mn, al * l_old + jnp.sum(pr, axis=-1, keepdims=True), al * acc_old + stacked3(pr, v, _dot)

    row = lax.broadcasted_iota(I32, (rows2, width), 0)
    col = lax.broadcasted_iota(I32, (rows2, width), 1)
    tok = jnp.where(row >= ts, row - ts, row)
    dist = (past - p * width) + tok - col
    distf = dist.astype(F32)
    stats = []
    for h in range(n_h):
        kcat = jnp.concatenate([k_refs[i][h] for i in range(pps)], axis=0)
        vcat = jnp.concatenate([v_refs[i][h] for i in range(pps)], axis=0)
        s = stacked3(qm_ref[h], kcat, _dot_nt) - float(slopes[h]) * distf
        stats.append(softmax_step(h, s, vcat, m_ref[h], l_ref[h], acc_ref[h]))
    for h in range(n_h):
        m_ref[h], l_ref[h], acc_ref[h] = stats[h]

    def update(h, s, v):
        m_ref[h], l_ref[h], acc_ref[h] = softmax_step(h, s, v, m_ref[h], l_ref[h], acc_ref[h])

    @pl.when(p == n_pages // pps - 1)
    def _():
        lam = _lambda_value(lq1, lk1, lq2, lk2, lam_init)
        rown = lax.broadcasted_iota(I32, (rows2, rows2), 0)
        coln = lax.broadcasted_iota(I32, (rows2, rows2), 1)
        tokn = jnp.where(rown >= ts, rown - ts, rown)
        distn = tokn - coln
        valid = (distn >= 0) & (coln < ts)
        for h in range(n_h):
            kn = qkv_ref[:, pl.ds(d_attn + h * HEAD_DIM, HEAD_DIM)]
            vn = qkv_ref[:, pl.ds(2 * d_attn + h * HEAD_DIM, HEAD_DIM)]
            s = stacked3(qm_ref[h], kn, _dot_nt)
            s = jnp.where(valid, s - float(slopes[h]) * distn.astype(F32), NEG_INF)
            update(h, s, vn)
            o = acc_ref[h] / l_ref[h]
            o = o - lam * pltpu.roll(o, ts, 0)
            o_ref[:, pl.ds(h * HEAD_DIM, HEAD_DIM)] = _subln(o, gain_ref[...], lam_init)


def _paged_attention(proj_s, cache_k, cache_v, layer, page_table, lw, lam_init, slopes, ts, tpad, d_attn):
    n_seq, n_pages = page_table.shape
    n_h = d_attn // HEAD_DIM
    pps = PAGES_PER_STEP
    assert n_pages % pps == 0
    vec = lambda n: pl.BlockSpec((1, n), lambda s, p, pt: (0, 0))
    cache_specs = [pl.BlockSpec((None, None, n_h, PAGE_SIZE, HEAD_DIM),
                                lambda s, p, pt, i=i: (layer, pt[s * n_pages + p * pps + i], 0, 0, 0))
                   for i in range(pps)]
    return pl.pallas_call(
        functools.partial(_paged_attn_kernel, ts=ts, n_h=n_h, n_pages=n_pages, pps=pps, lam_init=lam_init,
                          slopes=tuple(float(x) for x in slopes)),
        grid_spec=pltpu.PrefetchScalarGridSpec(
            num_scalar_prefetch=1,
            grid=(n_seq, n_pages // pps),
            in_specs=[vec(QK_HALF), vec(QK_HALF), vec(QK_HALF), vec(QK_HALF), vec(HEAD_DIM),
                      pl.BlockSpec((tpad, 3 * d_attn), lambda s, p, pt: (s, 0)),
                      *cache_specs, *cache_specs],
            out_specs=pl.BlockSpec((tpad, d_attn), lambda s, p, pt: (s, 0)),
            scratch_shapes=[pltpu.VMEM((n_h, 2 * ts, HEAD_DIM), F32),
                            pltpu.VMEM((n_h, 2 * ts, 1), F32),
                            pltpu.VMEM((n_h, 2 * ts, 1), F32),
                            pltpu.VMEM((n_h, 2 * ts, HEAD_DIM), F32)]),
        out_shape=jax.ShapeDtypeStruct((n_seq * tpad, d_attn), F32),
        name="paged_attention",
        compiler_params=_cparams(("parallel", "arbitrary"), 48),
    )(page_table.reshape(-1), lw['lambda_q1'].reshape(1, -1), lw['lambda_k1'].reshape(1, -1),
      lw['lambda_q2'].reshape(1, -1), lw['lambda_k2'].reshape(1, -1), lw['subln_gain'].reshape(1, -1),
      proj_s, *([cache_k] * pps), *([cache_v] * pps))


def _pool_kernel(u_ref, pre_ref, w_ref, sc_ref, o_ref, buf_ref, *, t, pos0, precise):
    g = pl.program_id(1)
    hist = pre_ref.shape[0]
    buf_ref[pl.ds(0, hist), :] = pre_ref[...]
    buf_ref[pl.ds(hist, t), :] = u_ref[...]
    pos = pos0 + 1 + lax.broadcasted_iota(I32, (t, 1), 0)
    for gi, w in enumerate(POOL_WINDOWS):
        @pl.when(g == gi)
        def _(w=w):
            acc = buf_ref[pl.ds(hist, t), :]
            for i in range(1, w):
                acc = acc + buf_ref[pl.ds(hist - i, t), :]
            cnt = jnp.minimum(pos, w).astype(F32)
            diff = acc / cnt - u_ref[...]
            if precise:
                mixed = _dot3(diff, w_ref[...])
            else:
                mixed = _dot(diff.astype(BF16), w_ref[...].astype(BF16))
            o_ref[...] = mixed * sc_ref[...]


def _pool_mix(proj, col0, prefix16, w_pool, scale, n_b, t, pos0, precise):
    n_g = len(POOL_WINDOWS)
    grp = w_pool.shape[-1]
    cb = col0 // grp
    return pl.pallas_call(
        functools.partial(_pool_kernel, t=t, pos0=pos0, precise=precise),
        grid=(n_b, n_g),
        in_specs=[pl.BlockSpec((t, grp), lambda b, g: (b, cb + g)),
                  pl.BlockSpec((None, 16, grp), lambda b, g: (b, 0, g)),
                  pl.BlockSpec((None, grp, grp), lambda b, g: (g, 0, 0)),
                  pl.BlockSpec((1, grp), lambda b, g: (0, g))],
        out_specs=pl.BlockSpec((t, grp), lambda b, g: (b, g)),
        out_shape=jax.ShapeDtypeStruct((n_b * t, n_g * grp), F32),
        scratch_shapes=[pltpu.VMEM((16 + t, grp), F32)],
        name="pool_mix",
        compiler_params=_cparams(("parallel", "parallel"), 32),
    )(proj, prefix16, w_pool, scale.reshape(1, -1))


def _dprep_kernel(x_ref, pre_ref, cw_ref, o_ref, buf_ref, *, t, n_h):
    j = pl.program_id(1)
    buf_ref[pl.ds(0, 8), :] = pre_ref[...]
    buf_ref[pl.ds(8, t), :] = x_ref[...]
    acc = buf_ref[pl.ds(8 - (CONV_W - 1), t), :] * cw_ref[pl.ds(0, 1), :]
    for i in range(1, CONV_W):
        acc = acc + buf_ref[pl.ds(8 - (CONV_W - 1) + i, t), :] * cw_ref[pl.ds(i, 1), :]
    s = _silu(acc)
    nrm = s * lax.rsqrt(jnp.sum(s * s, axis=-1, keepdims=True) + RMS_EPS)
    o_ref[...] = jnp.where(j < n_h, nrm * (HEAD_DIM ** -0.5), jnp.where(j < 2 * n_h, nrm, s))


def _delta_prep(proj, col0, prefix8, conv_w, n_b, t, d_delta):
    n_h = d_delta // HEAD_DIM
    cb = col0 // HEAD_DIM
    return pl.pallas_call(
        functools.partial(_dprep_kernel, t=t, n_h=n_h),
        grid=(n_b, 3 * n_h),
        in_specs=[pl.BlockSpec((t, HEAD_DIM), lambda b, j: (b, cb + j)),
                  pl.BlockSpec((None, 8, HEAD_DIM), lambda b, j: (b, 0, j)),
                  pl.BlockSpec((CONV_W, HEAD_DIM), lambda b, j: (0, j))],
        out_specs=pl.BlockSpec((t, HEAD_DIM), lambda b, j: (b, j)),
        out_shape=jax.ShapeDtypeStruct((n_b * t, 3 * d_delta), F32),
        scratch_shapes=[pltpu.VMEM((8 + t, HEAD_DIM), F32)],
        name="delta_prep",
        compiler_params=_cparams(("parallel", "parallel"), 32),
    )(proj, prefix8, conv_w)


def _gates_kernel(x_ref, alog_ref, dtb_ref, o_ref, *, n_h):
    x = x_ref[...]
    lane = lax.broadcasted_iota(I32, x.shape, 1)
    beta = _sigmoid(x)
    z = x + dtb_ref[...]
    softplus = jnp.maximum(z, 0.0) + jnp.log(1.0 + jnp.exp(-jnp.abs(z)))
    g = -jnp.exp(alog_ref[...]) * softplus
    o_ref[...] = jnp.where(lane < n_h, beta, g)


def _gates(proj, col0, a_log, dt_bias, rows, tm, n_h):
    cb = col0 // LANES
    pad = lambda v: jnp.zeros((1, LANES), F32).at[0, n_h:2 * n_h].set(v.astype(F32))
    return pl.pallas_call(
        functools.partial(_gates_kernel, n_h=n_h),
        grid=(rows // tm,),
        in_specs=[pl.BlockSpec((tm, LANES), lambda i: (i, cb)),
                  pl.BlockSpec((1, LANES), lambda i: (0, 0)),
                  pl.BlockSpec((1, LANES), lambda i: (0, 0))],
        out_specs=pl.BlockSpec((tm, LANES), lambda i: (i, 0)),
        out_shape=jax.ShapeDtypeStruct((rows, LANES), F32),
        name="delta_gates",
        compiler_params=_cparams(("parallel",), 32),
    )(proj, pad(a_log), pad(dt_bias))


def _unit_lower_inverse(lm, c, mm):
    ii = lax.broadcasted_iota(I32, (c, c), 0)
    jj = lax.broadcasted_iota(I32, (c, c), 1)
    eye = (ii == jj).astype(F32)
    base = DELTA_BASE
    shift = int(math.log2(base))
    n0 = -jnp.where((ii >> shift) == (jj >> shift), lm, 0.0)
    inv = eye + n0
    npow = n0
    for _ in range(shift - 1):
        npow = mm(npow, npow)
        inv = inv + mm(inv, npow)
    size = base
    while size < c:
        s = int(math.log2(size))
        sib = ((ii >> (s + 1)) == (jj >> (s + 1))) & ((ii >> s) != (jj >> s))
        loff = jnp.where(sib, lm, 0.0)
        inv = inv - mm(inv, mm(loff, inv))
        size *= 2
    return inv


def _delta_kernel(q_ref, k_ref, v_ref, *rest, c, n_h, n_z, precise):
    z_refs = rest[:n_z]
    gt_ref, gr_ref, nrm_ref, s0_ref, o_ref, sf_ref, st_ref = rest[n_z:]
    heads_per_z = n_h // n_z
    ci = pl.program_id(1)
    if precise:
        mm, mm_nt, mm_tn = _dot3, functools.partial(_dot3, dot=_dot_nt), functools.partial(_dot3, dot=_dot_tn)
    else:
        cast = lambda f: (lambda a, b: f(a.astype(BF16), b.astype(BF16)))
        mm, mm_nt, mm_tn = cast(_dot), cast(_dot_nt), cast(_dot_tn)
    mm_inv = mm
    ii = lax.broadcasted_iota(I32, (c, c), 0)
    jj = lax.broadcasted_iota(I32, (c, c), 1)
    incl = ii >= jj
    strict = ii > jj

    @pl.when(ci == 0)
    def _():
        st_ref[...] = s0_ref[...]

    gt = gt_ref[...]
    gth, gtl = _split_bf16(gt)
    tri = incl.astype(BF16)
    cum_t = _dot(tri, gth) + _dot(tri, gtl)
    grh, grl = _split_bf16(gr_ref[...])
    upper = (ii <= jj).astype(BF16)
    cum_r = _dot(grh, upper) + _dot(grl, upper)

    nrm = nrm_ref[...]
    outs, states = [], []
    for h in range(n_h):
        cols = pl.ds(h * HEAD_DIM, HEAD_DIM)
        q, k, v = q_ref[:, cols], k_ref[:, cols], v_ref[:, cols]
        beta = gt[:, h:h + 1]
        cum = cum_t[:, n_h + h:n_h + h + 1]
        cum_last = cum[c - 1:c, :]
        decay = jnp.exp(jnp.where(incl, cum - cum_r[h:h + 1, :], NEG_INF))
        ecum = jnp.exp(cum)
        kk = mm_nt(k, k)
        lower = jnp.where(strict, beta * kk * decay, 0.0)
        inv = _unit_lower_inverse(lower, c, mm_inv)
        rhs = jnp.concatenate([v * beta, k * (beta * ecum)], axis=1)
        sol = mm_inv(inv, rhs)
        new_v = sol[:, :HEAD_DIM]
        k_cum = sol[:, HEAD_DIM:]
        qk = mm_nt(q, k) * decay
        q_dec = q * ecum
        k_dec = k * jnp.exp(cum_last - cum)
        g_tot = jnp.exp(cum_last)
        state = st_ref[h]
        w = new_v - mm(k_cum, state)
        out = mm(q_dec, state) + mm(qk, w)
        states.append(state * g_tot + mm_tn(k_dec, w))
        out = out * lax.rsqrt(jnp.mean(out * out, axis=-1, keepdims=True) + RMS_EPS) * nrm
        z = z_refs[h // heads_per_z][:, pl.ds((h % heads_per_z) * HEAD_DIM, HEAD_DIM)]
        outs.append(out * _silu(z))
    for h in range(n_h):
        o_ref[:, pl.ds(h * HEAD_DIM, HEAD_DIM)] = outs[h]
        st_ref[h] = states[h]

    @pl.when(ci == pl.num_programs(1) - 1)
    def _():
        sf_ref[...] = st_ref[...]


def _delta_rule(dq, proj, zcol0, gates, delta_norm, s0, n_b, t, c, d_delta, precise):
    n_h = d_delta // HEAD_DIM
    nc = t // c
    gr = gates[:, n_h:2 * n_h].reshape(n_b, nc, c, n_h).transpose(0, 1, 3, 2)
    zw = math.gcd(zcol0, d_delta)
    rows = lambda off: pl.BlockSpec((c, d_delta), lambda b, i, off=off: (b * nc + i, off))
    zspecs = [pl.BlockSpec((c, zw), lambda b, i, j=j: (b * nc + i, zcol0 // zw + j)) for j in range(d_delta // zw)]
    state = pl.BlockSpec((None, n_h, HEAD_DIM, HEAD_DIM), lambda b, i: (b, 0, 0, 0))
    return pl.pallas_call(
        functools.partial(_delta_kernel, c=c, n_h=n_h, n_z=len(zspecs), precise=precise),
        grid=(n_b, nc),
        in_specs=[rows(0), rows(1), rows(2), *zspecs,
                  pl.BlockSpec((c, LANES), lambda b, i: (b * nc + i, 0)),
                  pl.BlockSpec((None, None, n_h, c), lambda b, i: (b, i, 0, 0)),
                  pl.BlockSpec((1, HEAD_DIM), lambda b, i: (0, 0)),
                  state],
        out_specs=[rows(0), state],
        out_shape=[jax.ShapeDtypeStruct((n_b * t, d_delta), F32),
                   jax.ShapeDtypeStruct((n_b, n_h, HEAD_DIM, HEAD_DIM), F32)],
        scratch_shapes=[pltpu.VMEM((n_h, HEAD_DIM, HEAD_DIM), F32)],
        name="delta_rule",
        compiler_params=_cparams(("parallel", "arbitrary"), 32),
    )(dq, dq, dq, *([proj] * len(zspecs)), gates, gr, delta_norm.reshape(1, -1), s0)


def _router_kernel(h_ref, g_ref, wr_ref, xn_ref, meta_ref, cnt_ref, *, tm):
    @pl.when(pl.program_id(0) == 0)
    def _():
        cnt_ref[...] = jnp.zeros(cnt_ref.shape, F32)

    x = h_ref[...]
    xn = x * lax.rsqrt(jnp.mean(x * x, axis=-1, keepdims=True) + RMS_EPS) * g_ref[...]
    xn_ref[...] = xn
    logits = _dot3(xn, wr_ref[...])
    lane = lax.broadcasted_iota(I32, (tm, LANES), 1)
    big = jnp.int32(LANES)

    def first_max(vals, valid):
        vmax = jnp.max(jnp.where(valid, vals, -jnp.inf), axis=-1, keepdims=True)
        idx = jnp.min(jnp.where(valid & (vals == vmax), lane, big), axis=-1, keepdims=True)
        return vmax, idx

    gvalid = lane < N_GROUPS
    gmax, gstar = first_max(logits, gvalid)
    p_sel = 1.0 / jnp.sum(jnp.where(gvalid, jnp.exp(logits - gmax), 0.0), axis=-1, keepdims=True)
    e0 = N_GROUPS + gstar * EXPERTS_PER_GROUP
    evalid = (lane >= e0) & (lane < e0 + EXPERTS_PER_GROUP)
    emax, _ = first_max(logits, evalid)
    pe = jnp.where(evalid, jnp.exp(logits - emax), 0.0)
    pe = pe / jnp.sum(pe, axis=-1, keepdims=True)
    v1, i1 = first_max(pe, evalid)
    v2, i2 = first_max(pe, evalid & (lane != i1))
    den = v1 + v2
    g1 = p_sel * v1 / den
    g2 = p_sel * v2 / den
    ex1 = i1 - N_GROUPS
    ex2 = i2 - N_GROUPS
    oh1 = lane == ex1
    oh2 = lane == ex2
    oh = jnp.where(oh1 | oh2, 1.0, 0.0)
    ri = lax.broadcasted_iota(I32, (tm, tm), 0)
    rj = lax.broadcasted_iota(I32, (tm, tm), 1)
    before = _dot((ri > rj).astype(BF16), oh.astype(BF16)) + cnt_ref[...]
    r1 = jnp.sum(jnp.where(oh1, before, 0.0), axis=-1, keepdims=True)
    r2 = jnp.sum(jnp.where(oh2, before, 0.0), axis=-1, keepdims=True)
    cnt_ref[...] = cnt_ref[...] + jnp.sum(oh, axis=0, keepdims=True)
    meta = jnp.where(lane == 0, ex1.astype(F32),
           jnp.where(lane == 1, ex2.astype(F32),
           jnp.where(lane == 2, g1,
           jnp.where(lane == 3, g2,
           jnp.where(lane == 4, r1,
           jnp.where(lane == 5, r2, 0.0))))))
    meta_ref[...] = meta


def _router(h, gain, w_rg, w_re, tm):
    m, d = h.shape
    wr = jnp.zeros((d, LANES), F32)
    wr = wr.at[:, :N_GROUPS].set(w_rg)
    wr = wr.at[:, N_GROUPS:N_GROUPS + N_EXPERTS].set(w_re.transpose(1, 0, 2).reshape(d, N_EXPERTS))
    return pl.pallas_call(
        functools.partial(_router_kernel, tm=tm),
        grid=(m // tm,),
        in_specs=[pl.BlockSpec((tm, d), lambda i: (i, 0)),
                  pl.BlockSpec((1, d), lambda i: (0, 0)),
                  pl.BlockSpec((d, LANES), lambda i: (0, 0))],
        out_specs=[pl.BlockSpec((tm, d), lambda i: (i, 0)),
                   pl.BlockSpec((tm, LANES), lambda i: (i, 0)),
                   pl.BlockSpec((1, LANES), lambda i: (0, 0))],
        out_shape=[jax.ShapeDtypeStruct((m, d), F32),
                   jax.ShapeDtypeStruct((m, LANES), F32),
                   jax.ShapeDtypeStruct((1, LANES), F32)],
        name="router",
        compiler_params=_cparams(("arbitrary",), 56),
    )(h, gain.reshape(1, d), wr)


def _row_copy(src_hbm, dst, sem, src_row, dst_row):
    return pltpu.make_async_copy(src_hbm.at[pl.ds(src_row, 1), :], dst.at[pl.ds(dst_row, 1), :], sem)


def _gather_kernel(tok_ref, used_ref, x_hbm, o_ref, buf_ref, sem, *, rows):
    base = pl.program_id(0) * rows

    @pl.when(base < used_ref[0])
    def _():
        def issue(r, c):
            _row_copy(x_hbm, buf_ref, sem, tok_ref[base + r], r).start()
            return c
        lax.fori_loop(0, rows, issue, 0)
        pltpu.make_async_copy(x_hbm.at[pl.ds(0, rows), :], buf_ref, sem).wait()
        o_ref[...] = buf_ref[...].astype(o_ref.dtype)


def _dispatch(xn, tok_of_row, used, n_rows, rows, out_dtype):
    d = xn.shape[1]
    return pl.pallas_call(
        functools.partial(_gather_kernel, rows=rows),
        grid_spec=pltpu.PrefetchScalarGridSpec(
            num_scalar_prefetch=2,
            grid=(n_rows // rows,),
            in_specs=[pl.BlockSpec(memory_space=pl.ANY)],
            out_specs=pl.BlockSpec((rows, d), lambda i, tok, used: (i, 0)),
            scratch_shapes=[pltpu.VMEM((rows, d), F32), pltpu.SemaphoreType.DMA(())]),
        out_shape=jax.ShapeDtypeStruct((n_rows, d), out_dtype),
        name="dispatch",
        compiler_params=_cparams(("arbitrary",), 32),
    )(tok_of_row, used, xn)


def _for_valid_rows(nrows, total, sub, fn):
    need = (nrows + sub - 1) // sub
    for q in range(1, total // sub + 1):
        @pl.when(need == q)
        def _(q=q):
            fn(q * sub)


def _expert_up_kernel(be_ref, br_ref, x_ref, wg_ref, wu_ref, h_ref, *, precise):
    w = pl.program_id(1)

    def run(rows):
        x = x_ref[pl.ds(0, rows), :]
        if precise:
            hg = _dot3(x, wg_ref[...])
            hu = _dot3(x, wu_ref[...])
        else:
            hg = _dot(x, wg_ref[...].astype(BF16))
            hu = _dot(x, wu_ref[...].astype(BF16))
        h_ref[pl.ds(0, rows), :] = (_silu(hg) * hu).astype(h_ref.dtype)

    _for_valid_rows(br_ref[w], x_ref.shape[0], MOE_SUB, run)


def _expert_down_kernel(be_ref, br_ref, h_ref, wd_ref, y_ref, *, precise):
    w = pl.program_id(1)

    def run(rows):
        hid = h_ref[pl.ds(0, rows), :]
        if precise:
            y_ref[pl.ds(0, rows), :] = _dot3(hid, wd_ref[...])
        else:
            y_ref[pl.ds(0, rows), :] = _dot(hid, wd_ref[...].astype(BF16))

    _for_valid_rows(br_ref[w], h_ref.shape[0], MOE_SUB, run)


def _experts(xs, blk_e, blk_rows, w_gate, w_up, w_down, layer, r, precise):
    n_rows, d = xs.shape
    nb = n_rows // r
    d_e = w_gate.shape[-1]
    tc, tn = MOE_HID_TILE, MOE_OUT_TILE
    hid = pl.pallas_call(
        functools.partial(_expert_up_kernel, precise=precise),
        grid_spec=pltpu.PrefetchScalarGridSpec(
            num_scalar_prefetch=2,
            grid=(d_e // tc, nb),
            in_specs=[pl.BlockSpec((r, d), lambda c, w, be, br: (w, 0)),
                      pl.BlockSpec((None, None, d, tc), lambda c, w, be, br: (layer, be[w], 0, c)),
                      pl.BlockSpec((None, None, d, tc), lambda c, w, be, br: (layer, be[w], 0, c))],
            out_specs=pl.BlockSpec((r, tc), lambda c, w, be, br: (w, c))),
        out_shape=jax.ShapeDtypeStruct((n_rows, d_e), F32 if precise else BF16),
        name="experts_up",
        compiler_params=_cparams(("arbitrary", "arbitrary"), 48),
    )(blk_e, blk_rows, xs, w_gate, w_up)
    return pl.pallas_call(
        functools.partial(_expert_down_kernel, precise=precise),
        grid_spec=pltpu.PrefetchScalarGridSpec(
            num_scalar_prefetch=2,
            grid=(d // tn, nb),
            in_specs=[pl.BlockSpec((r, d_e), lambda n, w, be, br: (w, 0)),
                      pl.BlockSpec((None, None, d_e, tn), lambda n, w, be, br: (layer, be[w], 0, n))],
            out_specs=pl.BlockSpec((r, tn), lambda n, w, be, br: (w, n))),
        out_shape=jax.ShapeDtypeStruct((n_rows, d), F32),
        name="experts_down",
        compiler_params=_cparams(("arbitrary", "arbitrary"), 48),
    )(blk_e, blk_rows, hid, w_down)


def _combine_kernel(dest_ref, y_hbm, h_ref, meta_ref, gf_ref, o_ref, buf_ref, sem, *, rows, final):
    base = pl.program_id(0) * rows

    def issue(r, c):
        t2 = 2 * (base + r)
        _row_copy(y_hbm, buf_ref.at[0], sem, dest_ref[t2], r).start()
        _row_copy(y_hbm, buf_ref.at[1], sem, dest_ref[t2 + 1], r).start()
        return c
    lax.fori_loop(0, rows, issue, 0)
    pltpu.make_async_copy(y_hbm.at[pl.ds(0, rows), :], buf_ref.at[0], sem).wait()
    pltpu.make_async_copy(y_hbm.at[pl.ds(0, rows), :], buf_ref.at[1], sem).wait()
    meta = meta_ref[...]
    out = h_ref[...] + (buf_ref[0] * meta[:, 2:3] + buf_ref[1] * meta[:, 3:4])
    if final:
        out = out * lax.rsqrt(jnp.mean(out * out, axis=-1, keepdims=True) + RMS_EPS) * gf_ref[...]
    o_ref[...] = out


def _combine(y, dest, h, meta, norm_final, rows, final):
    m, d = h.shape
    return pl.pallas_call(
        functools.partial(_combine_kernel, rows=rows, final=final),
        grid_spec=pltpu.PrefetchScalarGridSpec(
            num_scalar_prefetch=1,
            grid=(m // rows,),
            in_specs=[pl.BlockSpec(memory_space=pl.ANY),
                      pl.BlockSpec((rows, d), lambda i, dest: (i, 0)),
                      pl.BlockSpec((rows, LANES), lambda i, dest: (i, 0)),
                      pl.BlockSpec((1, d), lambda i, dest: (0, 0))],
            out_specs=pl.BlockSpec((rows, d), lambda i, dest: (i, 0)),
            scratch_shapes=[pltpu.VMEM((2, rows, d), F32), pltpu.SemaphoreType.DMA(())]),
        out_shape=jax.ShapeDtypeStruct((m, d), F32),
        name="combine",
        compiler_params=_cparams(("arbitrary",), 48),
    )(dest, y, h, meta, norm_final.reshape(1, d))


def _moe(h, norm_ffn, w_rg, w_re, w_gate, w_up, w_down, norm_final, *, layer, final, precise, r, token_rows,
         gather_rows):
    m, d = h.shape
    xn, meta, counts = _router(h, norm_ffn, w_rg, w_re, token_rows)
    experts = meta[:, 0:2].astype(I32)
    rank = meta[:, 4:6].astype(I32)
    cnt = counts[0, :N_EXPERTS].astype(I32)
    padded = (cnt + r - 1) // r * r
    pad_end = jnp.cumsum(padded)
    pad_start = pad_end - padded
    dest = pad_start[experts] + rank
    nb = -(-(m * TOP_K) // r) + N_EXPERTS
    blk_start = jnp.arange(nb, dtype=I32) * r
    used = pad_end[-1]
    active = blk_start < used
    blk_e = jnp.minimum(jnp.sum(pad_end[None, :] <= blk_start[:, None], axis=1), N_EXPERTS - 1).astype(I32)
    blk_rows = jnp.where(active, jnp.clip(cnt[blk_e] - (blk_start - pad_start[blk_e]), 0, r), 0).astype(I32)
    last_e = blk_e[jnp.maximum(used // r - 1, 0)]
    blk_e = jnp.where(active, blk_e, last_e)
    tok_of_row = jnp.zeros((nb * r,), I32).at[dest.reshape(-1)].set(
        jnp.repeat(jnp.arange(m, dtype=I32), TOP_K))
    xs = _dispatch(xn, tok_of_row, used.reshape(1).astype(I32), nb * r, gather_rows, F32 if precise else BF16)
    y = _experts(xs, blk_e, blk_rows, w_gate, w_up, w_down, layer, r, precise)
    return _combine(y, dest.reshape(-1).astype(I32), h, meta, norm_final, token_rows, final)


def kernel(x_prompt, x_sample, cache_k, cache_v, state_pool, state_conv, state_delta, page_table, meta, norm_mix, w_in, lambda_q1, lambda_k1, lambda_q2, lambda_k2, subln_gain, w_pool, pool_scale, conv_w, a_log, dt_bias, delta_norm, w_out, norm_ffn, w_router_group, w_router_expert, w_gate, w_up, w_down, norm_final):
    depth = w_in.shape[0]
    n_b, seq, d_model = x_prompt.shape
    n_s, ts, _ = x_sample.shape
    tp = N_META + seq
    tpad = SUBLANES
    n_mix = d_model // HEAD_DIM
    d_attn = (3 * n_mix) // 8 * HEAD_DIM
    d_delta = d_attn
    d_pool = d_model - d_attn - d_delta
    n_dh = d_delta // HEAD_DIM
    d_in = w_in.shape[-1]
    col_u = 3 * d_attn
    col_c = col_u + d_pool
    col_z = col_c + 3 * d_delta
    col_ba = col_z + d_delta
    d_in_pad = -(-d_in // PROJ_COL_TILE) * PROJ_COL_TILE
    mp = n_b * tp
    past = page_table.shape[1] * PAGE_SIZE
    slopes = _alibi_slopes(d_attn // HEAD_DIM)

    cache_kh = cache_k.transpose(0, 1, 3, 2, 4)
    cache_vh = cache_v.transpose(0, 1, 3, 2, 4)
    w_in_t = w_in.transpose(0, 2, 1)
    w_in_tb = w_in_t.astype(BF16)
    w_out_b = w_out.astype(BF16)

    hp = jnp.concatenate([jnp.broadcast_to(meta[None], (n_b, N_META, d_model)), x_prompt], axis=1)
    xp = hp.reshape(mp, d_model)
    xs = x_sample.reshape(n_s * ts, d_model)

    def pad_rows(a):
        a = a.reshape(n_s, ts, a.shape[-1])
        return jnp.pad(a, ((0, 0), (0, tpad - ts), (0, 0))).reshape(n_s * tpad, a.shape[-1])

    def real_rows(a):
        return a.reshape(n_s, tpad, -1)[:, :ts].reshape(n_s * ts, -1)

    def mixers(proj, oa, pool_prefix, conv_prefix, s0, n_seq, t, chunk, pos0, live, precise):
        ob = _pool_mix(proj, col_u, pool_prefix, w_pool[l], pool_scale[l], n_seq, t, pos0, precise)
        dq = _delta_prep(proj, col_c, conv_prefix, conv_w[l], n_seq, t, d_delta)
        gates = _gates(proj, col_ba, a_log[l], dt_bias[l], n_seq * t, t, n_dh)
        if live is not None:
            gates = jnp.where(live, gates, 0.0)
        oc, sf = _delta_rule(dq, proj, col_z, gates, delta_norm[l], s0, n_seq, t, chunk, d_delta, precise)
        return jnp.concatenate([oa, ob, oc], axis=1), sf

    moe_p = dict(precise=False, r=MOE_ROWS, token_rows=PROMPT_TOKEN_ROWS, gather_rows=GATHER_ROWS)
    moe_s = dict(precise=True, r=MOE_SUB, token_rows=n_s * ts, gather_rows=MOE_SUB)

    outs = {k: [] for k in ('kp', 'vp', 'pp', 'cp', 'dp', 'ks', 'vs', 'ps', 'cs', 'ds')}
    for l in range(depth):
        lw = {'lambda_q1': lambda_q1[l], 'lambda_k1': lambda_k1[l], 'lambda_q2': lambda_q2[l],
              'lambda_k2': lambda_k2[l], 'subln_gain': subln_gain[l]}
        lam_init = 0.8 - 0.6 * math.exp(-0.3 * l)
        final = l == depth - 1
        moe_w = (norm_ffn[l], w_router_group[l], w_router_expert[l], w_gate, w_up, w_down, norm_final)

        proj = _dense(xp, w_in_tb, gain=norm_mix[l], tm=PROMPT_ROW_TILE, layer=l, w_transposed=True)
        oa_p, k_p, v_p = _prompt_attention(proj, lw, lam_init, jnp.asarray(slopes), n_b, tp, d_attn)
        act, sf_p = mixers(
            proj, oa_p, jnp.zeros((n_b, 16, d_pool), F32), jnp.zeros((n_b, 8, 3 * d_delta), F32),
            jnp.zeros((n_b, n_dh, HEAD_DIM, HEAD_DIM), F32), n_b, tp, DELTA_CHUNK, 0, None, False)
        h = _dense(act, w_out_b, res=xp, tm=PROMPT_ROW_TILE, layer=l)
        xp = _moe(h, *moe_w, layer=l, final=final, **moe_p)

        proj_s32 = _dense(xs, w_in_t, gain=norm_mix[l], tm=n_s * ts, precise=True, layer=l,
                          w_transposed=True)
        proj_s3 = proj_s32.reshape(n_s, ts, d_in_pad)
        proj_s = pad_rows(proj_s32)
        pre_s = jnp.concatenate([jnp.zeros((n_s, 16 - POOL_BUF, d_pool), F32), state_pool[l]], axis=1)
        cpre_s = jnp.concatenate([jnp.zeros((n_s, 8 - (CONV_W - 1), 3 * d_delta), F32), state_conv[l]], axis=1)
        live = (jnp.arange(n_s * tpad) % tpad < ts)[:, None]
        oa_s = _paged_attention(proj_s, cache_kh, cache_vh, l, page_table, lw, lam_init, slopes, ts, tpad, d_attn)
        act_s, sf_s = mixers(proj_s, oa_s, pre_s, cpre_s, state_delta[l], n_s, tpad, tpad, past, live, True)
        h_s = _dense(real_rows(act_s), w_out, res=xs, tm=n_s * ts, precise=True, layer=l)
        xs = _moe(h_s, *moe_w, layer=l, final=final, **moe_s)

        pp3 = proj.reshape(n_b, tp, d_in_pad)
        n_ah = d_attn // HEAD_DIM
        outs['kp'].append(k_p.transpose(0, 2, 1, 3))
        outs['vp'].append(v_p.transpose(0, 2, 1, 3))
        outs['pp'].append(pp3[:, tp - POOL_BUF:, col_u:col_u + d_pool])
        outs['cp'].append(pp3[:, tp - (CONV_W - 1):, col_c:col_c + 3 * d_delta])
        outs['dp'].append(sf_p)
        outs['ks'].append(proj_s3[:, :, d_attn:2 * d_attn].reshape(n_s, ts, n_ah, HEAD_DIM))
        outs['vs'].append(proj_s3[:, :, 2 * d_attn:3 * d_attn].reshape(n_s, ts, n_ah, HEAD_DIM))
        u_s = proj_s3[:, :, col_u:col_u + d_pool]
        outs['ps'].append(jnp.concatenate([state_pool[l], u_s], axis=1)[:, -POOL_BUF:])
        c_s = proj_s3[:, :, col_c:col_c + 3 * d_delta]
        outs['cs'].append(jnp.concatenate([state_conv[l], c_s], axis=1)[:, -(CONV_W - 1):])
        outs['ds'].append(sf_s)

    y_prompt = xp.reshape(n_b, tp, d_model)[:, N_META:]
    y_sample = xs.reshape(n_s, ts, d_model)
    st = lambda k: jnp.stack(outs[k])
    return (y_prompt, y_sample, st('kp'), st('vp'), st('pp'), st('cp'), st('dp'),
            st('ks'), st('vs'), st('ps'), st('cs'), st('ds'))
```

```python
import functools
import math

import numpy as np
import jax
import jax.numpy as jnp
from jax import lax
from jax.experimental import pallas as pl
from jax.experimental.pallas import tpu as pltpu

F32 = jnp.float32
BF16 = jnp.bfloat16
I32 = jnp.int32

HEAD_DIM = 128
QK_HALF = HEAD_DIM // 2
POOL_WINDOWS = (2, 4, 8, 16)
POOL_BUF = 15
CONV_W = 4
N_META = 16
N_GROUPS = 4
EXPERTS_PER_GROUP = 8
N_EXPERTS = N_GROUPS * EXPERTS_PER_GROUP
TOP_K = 2
PAGE_SIZE = 128
RMS_EPS = 1e-6
NEG_INF = -1e30

LANES = 128
SUBLANES = 8
MIB = 1024 * 1024

PROMPT_ROW_TILE = 688
PROMPT_TOKEN_ROWS = 344
ALL_TOKEN_ROWS = 296
PROJ_COL_TILE = 512
ATTN_BLOCK = 256
DELTA_CHUNK = 48
DELTA_BASE = 8
PAGES_PER_STEP = 4
PAGED_HEAD_GROUP = 12
MOE_ROWS = 512
MOE_SUB = 128
MOE_HID_TILE = 256
MOE_OUT_TILE = 1024
GATHER_ROWS = 256


def _cparams(semantics, vmem_mib):
    return pltpu.CompilerParams(dimension_semantics=semantics, vmem_limit_bytes=vmem_mib * MIB)


def _alibi_slopes(n_heads):
    def pow2_slopes(m):
        start = 2.0 ** (-8.0 / m)
        return [start ** (i + 1) for i in range(m)]
    closest = 2 ** int(math.floor(math.log2(n_heads)))
    slopes = pow2_slopes(closest) + pow2_slopes(2 * closest)[0::2][: n_heads - closest]
    return np.array(slopes, dtype=np.float32)


def _sigmoid(x):
    return 1.0 / (1.0 + jnp.exp(-x))


def _silu(x):
    return x * _sigmoid(x)


def _split_bf16(x):
    hi = x.astype(BF16)
    lo = (x - hi.astype(F32)).astype(BF16)
    return hi, lo


def _stack_hilo(x):
    hi = x.astype(BF16).astype(F32)
    return jnp.concatenate([hi, x - hi], axis=0).astype(BF16)


def _dot(a, b):
    return jnp.dot(a, b, preferred_element_type=F32)


def _dot_nt(a, b):
    return lax.dot_general(a, b, (((1,), (1,)), ((), ())), preferred_element_type=F32)


def _dot_tn(a, b):
    return lax.dot_general(a, b, (((0,), (0,)), ((), ())), preferred_element_type=F32)


def _dot3(a, b, dot=_dot):
    ah, al = _split_bf16(a)
    bh, bl = _split_bf16(b)
    return dot(ah, bh) + (dot(ah, bl) + dot(al, bh))


def _proj_kernel(*refs, normed, residual, precise, w_transposed):
    mm = _dot_nt if w_transposed else _dot
    it = iter(refs)
    x_ref = next(it)
    g_ref = next(it) if normed else None
    w_ref = next(it)
    r_ref = next(it) if residual else None
    o_ref = next(it)
    xh_ref = next(it)
    xl_ref = next(it) if precise else None

    @pl.when(pl.program_id(1) == 0)
    def _():
        x = x_ref[...]
        if normed:
            ms = jnp.mean(x * x, axis=-1, keepdims=True)
            x = x * lax.rsqrt(ms + RMS_EPS) * g_ref[...]
        if precise:
            xh_ref[...], xl_ref[...] = _split_bf16(x)
        else:
            xh_ref[...] = x.astype(BF16)

    if precise:
        wh, wl = _split_bf16(w_ref[...])
        acc = mm(xh_ref[...], wh) + (mm(xh_ref[...], wl) + mm(xl_ref[...], wh))
    else:
        acc = mm(xh_ref[...], w_ref[...])
    if residual:
        acc = r_ref[...] + acc
    o_ref[...] = acc


def _dense(x, w, gain=None, res=None, *, tm, tn=PROJ_COL_TILE, precise=False, layer=None, w_transposed=False):
    m, d = x.shape
    n = w.shape[-2] if w_transposed else w.shape[-1]
    n_out = pl.cdiv(n, tn) * tn
    assert m % tm == 0
    in_specs = [pl.BlockSpec((tm, d), lambda i, j: (i, 0))]
    args = [x]
    if gain is not None:
        in_specs.append(pl.BlockSpec((1, d), lambda i, j: (0, 0)))
        args.append(gain.reshape(1, d))
    if layer is None:
        in_specs.append(pl.BlockSpec((d, tn), lambda i, j: (0, j)))
    elif w_transposed:
        in_specs.append(pl.BlockSpec((None, tn, d), lambda i, j: (layer, j, 0)))
    else:
        in_specs.append(pl.BlockSpec((None, d, tn), lambda i, j: (layer, 0, j)))
    args.append(w)
    if res is not None:
        in_specs.append(pl.BlockSpec((tm, tn), lambda i, j: (i, j)))
        args.append(res)
    return pl.pallas_call(
        functools.partial(_proj_kernel, normed=gain is not None, residual=res is not None, precise=precise,
                          w_transposed=w_transposed),
        grid=(m // tm, n_out // tn),
        in_specs=in_specs,
        out_specs=pl.BlockSpec((tm, tn), lambda i, j: (i, j)),
        out_shape=jax.ShapeDtypeStruct((m, n_out), F32),
        scratch_shapes=[pltpu.VMEM((tm, d), BF16)] * (2 if precise else 1),
        name="dense_precise" if precise else "dense",
        compiler_params=_cparams(("parallel", "arbitrary"), 56),
    )(*args)


def _lambda_value(lq1, lk1, lq2, lk2, lam_init):
    a = jnp.sum(lq1[...] * lk1[...], axis=-1, keepdims=True)
    b = jnp.sum(lq2[...] * lk2[...], axis=-1, keepdims=True)
    return jnp.exp(a) - jnp.exp(b) + lam_init


def _subln(o, gain, lam_init):
    o = o * lax.rsqrt(jnp.mean(o * o, axis=-1, keepdims=True) + RMS_EPS) * gain
    return o * (1.0 - lam_init)


def _prompt_attn_kernel(slopes_ref, lq1, lk1, lq2, lk2, gain_ref, q_ref, k_ref, v_ref, o_ref, ko_ref, vo_ref,
                        qp_ref, kp_ref, vp_ref, op_ref, *, t, tb, lam_init):
    h = pl.program_id(1)
    ko_ref[...] = k_ref[...]
    vo_ref[...] = v_ref[...]
    slope = slopes_ref[h]
    lam = _lambda_value(lq1, lk1, lq2, lk2, lam_init)
    tp = qp_ref.shape[0]
    nb = tp // tb
    for src, dst in ((q_ref, qp_ref), (k_ref, kp_ref), (v_ref, vp_ref)):
        dst[pl.ds(0, t), :] = src[...]
        dst[pl.ds(t, tp - t), :] = jnp.zeros((tp - t, HEAD_DIM), F32)
    dloc = lax.broadcasted_iota(I32, (tb, tb), 0) - lax.broadcasted_iota(I32, (tb, tb), 1)

    def qblock(qi, carry):
        r0 = pl.multiple_of(qi * tb, tb)
        q = qp_ref[pl.ds(r0, tb), :] * (QK_HALF ** -0.5)
        q1 = q[:, :QK_HALF].astype(BF16)
        q2 = q[:, QK_HALF:].astype(BF16)

        def kvblock(kj, c):
            m1, l1, a1, m2, l2, a2 = c
            c0 = pl.multiple_of(kj * tb, tb)
            kk = kp_ref[pl.ds(c0, tb), :]
            vv = vp_ref[pl.ds(c0, tb), :].astype(BF16)
            dist = dloc + (r0 - c0)
            causal = dist >= 0
            bias = slope * dist.astype(F32)

            s1 = _dot_nt(q1, kk[:, :QK_HALF].astype(BF16))
            s2 = _dot_nt(q2, kk[:, QK_HALF:].astype(BF16))
            s1 = jnp.where(causal, s1 - bias, NEG_INF)
            s2 = jnp.where(causal, s2 - bias, NEG_INF)
            n1 = jnp.maximum(m1, jnp.max(s1, axis=-1, keepdims=True))
            n2 = jnp.maximum(m2, jnp.max(s2, axis=-1, keepdims=True))
            p1 = jnp.exp(s1 - n1)
            p2 = jnp.exp(s2 - n2)
            al1 = jnp.exp(m1 - n1)
            al2 = jnp.exp(m2 - n2)
            pv1 = _dot(p1.astype(BF16), vv)
            pv2 = _dot(p2.astype(BF16), vv)
            l1 = al1 * l1 + jnp.sum(p1, axis=-1, keepdims=True)
            l2 = al2 * l2 + jnp.sum(p2, axis=-1, keepdims=True)
            return n1, l1, al1 * a1 + pv1, n2, l2, al2 * a2 + pv2

        m0 = jnp.full((tb, 1), NEG_INF, F32)
        z1 = jnp.zeros((tb, 1), F32)
        za = jnp.zeros((tb, HEAD_DIM), F32)
        m1, l1, a1, m2, l2, a2 = lax.fori_loop(0, qi + 1, kvblock, (m0, z1, za, m0, z1, za))
        o = a1 / l1 - lam * (a2 / l2)
        op_ref[pl.ds(r0, tb), :] = _subln(o, gain_ref[...], lam_init)
        return carry

    lax.fori_loop(0, nb, qblock, 0)
    o_ref[...] = op_ref[pl.ds(0, t), :]


def _prompt_attention(proj, lw, lam_init, slopes, n_b, t, d_attn):
    n_h = d_attn // HEAD_DIM
    tb = ATTN_BLOCK
    tp = -(-t // tb) * tb
    vec = lambda n: pl.BlockSpec((1, n), lambda b, h: (0, 0))
    blk = lambda off: pl.BlockSpec((t, HEAD_DIM), lambda b, h, off=off: (b, off + h))
    return pl.pallas_call(
        functools.partial(_prompt_attn_kernel, t=t, tb=tb, lam_init=lam_init),
        grid=(n_b, n_h),
        in_specs=[pl.BlockSpec(memory_space=pltpu.SMEM),
                  vec(QK_HALF), vec(QK_HALF), vec(QK_HALF), vec(QK_HALF), vec(HEAD_DIM),
                  blk(0), blk(n_h), blk(2 * n_h)],
        out_specs=[pl.BlockSpec((t, HEAD_DIM), lambda b, h: (b, h)),
                   pl.BlockSpec((None, None, t, HEAD_DIM), lambda b, h: (b, h, 0, 0)),
                   pl.BlockSpec((None, None, t, HEAD_DIM), lambda b, h: (b, h, 0, 0))],
        out_shape=[jax.ShapeDtypeStruct((n_b * t, d_attn), F32),
                   jax.ShapeDtypeStruct((n_b, n_h, t, HEAD_DIM), F32),
                   jax.ShapeDtypeStruct((n_b, n_h, t, HEAD_DIM), F32)],
        scratch_shapes=[pltpu.VMEM((tp, HEAD_DIM), F32)] * 4,
        name="prompt_attention",
        compiler_params=_cparams(("parallel", "parallel"), 40),
    )(slopes, lw['lambda_q1'].reshape(1, -1), lw['lambda_k1'].reshape(1, -1),
      lw['lambda_q2'].reshape(1, -1), lw['lambda_k2'].reshape(1, -1), lw['subln_gain'].reshape(1, -1),
      proj, proj, proj)


def _paged_attn_kernel(pt_ref, lq1, lk1, lq2, lk2, gain_ref, qkv_ref, *rest, ts, n_h, n_pages, pps, lam_init,
                       slopes):
    k_refs, v_refs = rest[:pps], rest[pps:2 * pps]
    o_ref, qm_ref, m_ref, l_ref, acc_ref = rest[2 * pps:]
    p = pl.program_id(1)
    d_attn = n_h * HEAD_DIM
    past = n_pages * PAGE_SIZE
    width = pps * PAGE_SIZE
    rows2 = 2 * ts

    @pl.when(p == 0)
    def _():
        lane = lax.broadcasted_iota(I32, (rows2, HEAD_DIM), 1)
        srow = lax.broadcasted_iota(I32, (rows2, HEAD_DIM), 0)
        for h in range(n_h):
            q = qkv_ref[:, pl.ds(h * HEAD_DIM, HEAD_DIM)] * (QK_HALF ** -0.5)
            qa = jnp.where(lane < QK_HALF, q, 0.0)
            qb = jnp.where(lane >= QK_HALF, pltpu.roll(q, ts, 0), 0.0)
            qm_ref[h] = jnp.where(srow < ts, qa, qb)
        m_ref[...] = jnp.full(m_ref.shape, NEG_INF, F32)
        l_ref[...] = jnp.zeros(l_ref.shape, F32)
        acc_ref[...] = jnp.zeros(acc_ref.shape, F32)

    def stacked3(lhs, rhs, dot):
        rh, rl = _split_bf16(rhs)
        both = dot(_stack_hilo(lhs), rh)
        return both[:rows2] + (both[rows2:] + dot(lhs.astype(BF16), rl))

    def softmax_step(h, s, v, m_old, l_old, acc_old):
        mn = jnp.maximum(m_old, jnp.max(s, axis=-1, keepdims=True))
        pr = jnp.exp(s - mn)
        al = jnp.exp(m_old - mn)
        return mn, al * l_old + jnp.sum(pr, axis=-1, keepdims=True), al * acc_old + stacked3(pr, v, _dot)

    row = lax.broadcasted_iota(I32, (rows2, width), 0)
    col = lax.broadcasted_iota(I32, (rows2, width), 1)
    tok = jnp.where(row >= ts, row - ts, row)
    dist = (past - p * width) + tok - col
    distf = dist.astype(F32)
    cat = lambda refs, h: jnp.concatenate([refs[i][h] for i in range(pps)], axis=0)
    for h0 in range(0, n_h, PAGED_HEAD_GROUP):
        heads = range(h0, min(h0 + PAGED_HEAD_GROUP, n_h))
        ksplit = {h: _split_bf16(cat(k_refs, h)) for h in heads}
        qstack = {h: _stack_hilo(qm_ref[h]) for h in heads}
        both = {h: _dot_nt(qstack[h], ksplit[h][0]) for h in heads}
        cross = {h: _dot_nt(qm_ref[h].astype(BF16), ksplit[h][1]) for h in heads}
        ss = {h: both[h][:rows2] + (both[h][rows2:] + cross[h]) - float(slopes[h]) * distf for h in heads}
        m_old = {h: m_ref[h] for h in heads}
        m_new = {h: jnp.maximum(m_old[h], jnp.max(ss[h], axis=-1, keepdims=True)) for h in heads}
        prs = {h: jnp.exp(ss[h] - m_new[h]) for h in heads}
        als = {h: jnp.exp(m_old[h] - m_new[h]) for h in heads}
        vsplit = {h: _split_bf16(cat(v_refs, h)) for h in heads}
        pstack = {h: _stack_hilo(prs[h]) for h in heads}
        pboth = {h: _dot(pstack[h], vsplit[h][0]) for h in heads}
        pcross = {h: _dot(prs[h].astype(BF16), vsplit[h][1]) for h in heads}
        for h in heads:
            l_ref[h] = als[h] * l_ref[h] + jnp.sum(prs[h], axis=-1, keepdims=True)
            acc_ref[h] = als[h] * acc_ref[h] + (pboth[h][:rows2] + (pboth[h][rows2:] + pcross[h]))
            m_ref[h] = m_new[h]

    def update(h, s, v):
        m_ref[h], l_ref[h], acc_ref[h] = softmax_step(h, s, v, m_ref[h], l_ref[h], acc_ref[h])

    @pl.when(p == n_pages // pps - 1)
    def _():
        lam = _lambda_value(lq1, lk1, lq2, lk2, lam_init)
        rown = lax.broadcasted_iota(I32, (rows2, rows2), 0)
        coln = lax.broadcasted_iota(I32, (rows2, rows2), 1)
        tokn = jnp.where(rown >= ts, rown - ts, rown)
        distn = tokn - coln
        valid = (distn >= 0) & (coln < ts)
        for h in range(n_h):
            kn = qkv_ref[:, pl.ds(d_attn + h * HEAD_DIM, HEAD_DIM)]
            vn = qkv_ref[:, pl.ds(2 * d_attn + h * HEAD_DIM, HEAD_DIM)]
            s = stacked3(qm_ref[h], kn, _dot_nt)
            s = jnp.where(valid, s - float(slopes[h]) * distn.astype(F32), NEG_INF)
            update(h, s, vn)
            o = acc_ref[h] / l_ref[h]
            o = o - lam * pltpu.roll(o, ts, 0)
            o_ref[:, pl.ds(h * HEAD_DIM, HEAD_DIM)] = _subln(o, gain_ref[...], lam_init)


def _paged_attention(proj_s, cache_k, cache_v, layer, page_table, lw, lam_init, slopes, ts, tpad, d_attn):
    n_seq, n_pages = page_table.shape
    n_h = d_attn // HEAD_DIM
    pps = PAGES_PER_STEP
    assert n_pages % pps == 0
    vec = lambda n: pl.BlockSpec((1, n), lambda s, p, pt: (0, 0))
    cache_specs = [pl.BlockSpec((None, None, n_h, PAGE_SIZE, HEAD_DIM),
                                lambda s, p, pt, i=i: (layer, pt[s * n_pages + p * pps + i], 0, 0, 0))
                   for i in range(pps)]
    return pl.pallas_call(
        functools.partial(_paged_attn_kernel, ts=ts, n_h=n_h, n_pages=n_pages, pps=pps, lam_init=lam_init,
                          slopes=tuple(float(x) for x in slopes)),
        grid_spec=pltpu.PrefetchScalarGridSpec(
            num_scalar_prefetch=1,
            grid=(n_seq, n_pages // pps),
            in_specs=[vec(QK_HALF), vec(QK_HALF), vec(QK_HALF), vec(QK_HALF), vec(HEAD_DIM),
                      pl.BlockSpec((tpad, 3 * d_attn), lambda s, p, pt: (s, 0)),
                      *cache_specs, *cache_specs],
            out_specs=pl.BlockSpec((tpad, d_attn), lambda s, p, pt: (s, 0)),
            scratch_shapes=[pltpu.VMEM((n_h, 2 * ts, HEAD_DIM), F32),
                            pltpu.VMEM((n_h, 2 * ts, 1), F32),
                            pltpu.VMEM((n_h, 2 * ts, 1), F32),
                            pltpu.VMEM((n_h, 2 * ts, HEAD_DIM), F32)]),
        out_shape=jax.ShapeDtypeStruct((n_seq * tpad, d_attn), F32),
        name="paged_attention",
        compiler_params=_cparams(("parallel", "arbitrary"), 48),
    )(page_table.reshape(-1), lw['lambda_q1'].reshape(1, -1), lw['lambda_k1'].reshape(1, -1),
      lw['lambda_q2'].reshape(1, -1), lw['lambda_k2'].reshape(1, -1), lw['subln_gain'].reshape(1, -1),
      proj_s, *([cache_k] * pps), *([cache_v] * pps))


def _pool_kernel(u_ref, pre_ref, w_ref, sc_ref, o_ref, buf_ref, *, t, pos0, precise):
    g = pl.program_id(1)
    hist = pre_ref.shape[0]
    buf_ref[pl.ds(0, hist), :] = pre_ref[...]
    buf_ref[pl.ds(hist, t), :] = u_ref[...]
    pos = pos0 + 1 + lax.broadcasted_iota(I32, (t, 1), 0)
    for gi, w in enumerate(POOL_WINDOWS):
        @pl.when(g == gi)
        def _(w=w):
            acc = buf_ref[pl.ds(hist, t), :]
            for i in range(1, w):
                acc = acc + buf_ref[pl.ds(hist - i, t), :]
            cnt = jnp.minimum(pos, w).astype(F32)
            diff = acc / cnt - u_ref[...]
            if precise:
                mixed = _dot3(diff, w_ref[...])
            else:
                mixed = _dot(diff.astype(BF16), w_ref[...].astype(BF16))
            o_ref[...] = mixed * sc_ref[...]


def _pool_mix(proj, col0, prefix16, w_pool, scale, n_b, t, pos0, precise):
    n_g = len(POOL_WINDOWS)
    grp = w_pool.shape[-1]
    cb = col0 // grp
    return pl.pallas_call(
        functools.partial(_pool_kernel, t=t, pos0=pos0, precise=precise),
        grid=(n_b, n_g),
        in_specs=[pl.BlockSpec((t, grp), lambda b, g: (b, cb + g)),
                  pl.BlockSpec((None, 16, grp), lambda b, g: (b, 0, g)),
                  pl.BlockSpec((None, grp, grp), lambda b, g: (g, 0, 0)),
                  pl.BlockSpec((1, grp), lambda b, g: (0, g))],
        out_specs=pl.BlockSpec((t, grp), lambda b, g: (b, g)),
        out_shape=jax.ShapeDtypeStruct((n_b * t, n_g * grp), F32),
        scratch_shapes=[pltpu.VMEM((16 + t, grp), F32)],
        name="pool_mix",
        compiler_params=_cparams(("parallel", "parallel"), 32),
    )(proj, prefix16, w_pool, scale.reshape(1, -1))


def _dprep_kernel(x_ref, pre_ref, cw_ref, o_ref, buf_ref, *, t, n_h):
    j = pl.program_id(1)
    buf_ref[pl.ds(0, 8), :] = pre_ref[...]
    buf_ref[pl.ds(8, t), :] = x_ref[...]
    acc = buf_ref[pl.ds(8 - (CONV_W - 1), t), :] * cw_ref[pl.ds(0, 1), :]
    for i in range(1, CONV_W):
        acc = acc + buf_ref[pl.ds(8 - (CONV_W - 1) + i, t), :] * cw_ref[pl.ds(i, 1), :]
    s = _silu(acc)
    nrm = s * lax.rsqrt(jnp.sum(s * s, axis=-1, keepdims=True) + RMS_EPS)
    o_ref[...] = jnp.where(j < n_h, nrm * (HEAD_DIM ** -0.5), jnp.where(j < 2 * n_h, nrm, s))


def _delta_prep(proj, col0, prefix8, conv_w, n_b, t, d_delta):
    n_h = d_delta // HEAD_DIM
    cb = col0 // HEAD_DIM
    return pl.pallas_call(
        functools.partial(_dprep_kernel, t=t, n_h=n_h),
        grid=(n_b, 3 * n_h),
        in_specs=[pl.BlockSpec((t, HEAD_DIM), lambda b, j: (b, cb + j)),
                  pl.BlockSpec((None, 8, HEAD_DIM), lambda b, j: (b, 0, j)),
                  pl.BlockSpec((CONV_W, HEAD_DIM), lambda b, j: (0, j))],
        out_specs=pl.BlockSpec((t, HEAD_DIM), lambda b, j: (b, j)),
        out_shape=jax.ShapeDtypeStruct((n_b * t, 3 * d_delta), F32),
        scratch_shapes=[pltpu.VMEM((8 + t, HEAD_DIM), F32)],
        name="delta_prep",
        compiler_params=_cparams(("parallel", "parallel"), 32),
    )(proj, prefix8, conv_w)


def _gates_kernel(x_ref, alog_ref, dtb_ref, o_ref, *, n_h):
    x = x_ref[...]
    lane = lax.broadcasted_iota(I32, x.shape, 1)
    beta = _sigmoid(x)
    z = x + dtb_ref[...]
    softplus = jnp.maximum(z, 0.0) + jnp.log(1.0 + jnp.exp(-jnp.abs(z)))
    g = -jnp.exp(alog_ref[...]) * softplus
    o_ref[...] = jnp.where(lane < n_h, beta, g)


def _gates(proj, col0, a_log, dt_bias, rows, tm, n_h):
    cb = col0 // LANES
    pad = lambda v: jnp.zeros((1, LANES), F32).at[0, n_h:2 * n_h].set(v.astype(F32))
    return pl.pallas_call(
        functools.partial(_gates_kernel, n_h=n_h),
        grid=(rows // tm,),
        in_specs=[pl.BlockSpec((tm, LANES), lambda i: (i, cb)),
                  pl.BlockSpec((1, LANES), lambda i: (0, 0)),
                  pl.BlockSpec((1, LANES), lambda i: (0, 0))],
        out_specs=pl.BlockSpec((tm, LANES), lambda i: (i, 0)),
        out_shape=jax.ShapeDtypeStruct((rows, LANES), F32),
        name="delta_gates",
        compiler_params=_cparams(("parallel",), 32),
    )(proj, pad(a_log), pad(dt_bias))


def _unit_lower_inverses(lms, c, mm):
    ii = lax.broadcasted_iota(I32, (c, c), 0)
    jj = lax.broadcasted_iota(I32, (c, c), 1)
    eye = (ii == jj).astype(F32)
    base = DELTA_BASE
    shift = int(math.log2(base))
    same = (ii >> shift) == (jj >> shift)
    npows = [-jnp.where(same, lm, 0.0) for lm in lms]
    invs = [eye + n0 for n0 in npows]
    for _ in range(shift - 1):
        npows = [mm(n, n) for n in npows]
        invs = [inv + mm(inv, n) for inv, n in zip(invs, npows)]
    size = base
    while size < c:
        s = int(math.log2(size))
        sib = ((ii >> (s + 1)) == (jj >> (s + 1))) & ((ii >> s) != (jj >> s))
        tmps = [mm(jnp.where(sib, lm, 0.0), inv) for lm, inv in zip(lms, invs)]
        invs = [inv - mm(inv, tmp) for inv, tmp in zip(invs, tmps)]
        size *= 2
    return invs


def _delta_kernel(q_ref, k_ref, v_ref, *rest, c, n_h, n_z, precise):
    z_refs = rest[:n_z]
    gt_ref, gr_ref, nrm_ref, s0_ref, o_ref, sf_ref, st_ref = rest[n_z:]
    heads_per_z = n_h // n_z
    ci = pl.program_id(1)
    if precise:
        mm, mm_nt, mm_tn = _dot3, functools.partial(_dot3, dot=_dot_nt), functools.partial(_dot3, dot=_dot_tn)
    else:
        cast = lambda f: (lambda a, b: f(a.astype(BF16), b.astype(BF16)))
        mm, mm_nt, mm_tn = cast(_dot), cast(_dot_nt), cast(_dot_tn)
    mm_inv = mm
    ii = lax.broadcasted_iota(I32, (c, c), 0)
    jj = lax.broadcasted_iota(I32, (c, c), 1)
    incl = ii >= jj
    strict = ii > jj

    @pl.when(ci == 0)
    def _():
        st_ref[...] = s0_ref[...]

    gt = gt_ref[...]
    gth, gtl = _split_bf16(gt)
    tri = incl.astype(BF16)
    cum_t = _dot(tri, gth) + _dot(tri, gtl)
    grh, grl = _split_bf16(gr_ref[...])
    upper = (ii <= jj).astype(BF16)
    cum_r = _dot(grh, upper) + _dot(grl, upper)

    nrm = nrm_ref[...]
    heads = range(n_h)
    col = lambda h: pl.ds(h * HEAD_DIM, HEAD_DIM)
    qs = [q_ref[:, col(h)] for h in heads]
    ks = [k_ref[:, col(h)] for h in heads]
    vs = [v_ref[:, col(h)] for h in heads]
    s_old = [st_ref[h] for h in heads]
    betas = [gt[:, h:h + 1] for h in heads]
    cums = [cum_t[:, n_h + h:n_h + h + 1] for h in heads]
    lasts = [cum[c - 1:c, :] for cum in cums]
    decays = [jnp.exp(jnp.where(incl, cums[h] - cum_r[h:h + 1, :], NEG_INF)) for h in heads]
    ecums = [jnp.exp(cum) for cum in cums]
    kks = [mm_nt(k, k) for k in ks]
    qks = [mm_nt(q, k) for q, k in zip(qs, ks)]
    lowers = [jnp.where(strict, betas[h] * kks[h] * decays[h], 0.0) for h in heads]
    invs = _unit_lower_inverses(lowers, c, mm_inv)
    rhss = [jnp.concatenate([vs[h] * betas[h], ks[h] * (betas[h] * ecums[h])], axis=1) for h in heads]
    sols = [mm_inv(inv, rhs) for inv, rhs in zip(invs, rhss)]
    ws = [sols[h][:, :HEAD_DIM] - mm(sols[h][:, HEAD_DIM:], s_old[h]) for h in heads]
    outs = [mm(qs[h] * ecums[h], s_old[h]) + mm(qks[h] * decays[h], ws[h]) for h in heads]
    upds = [mm_tn(ks[h] * jnp.exp(lasts[h] - cums[h]), ws[h]) for h in heads]
    for h in heads:
        out = outs[h] * lax.rsqrt(jnp.mean(outs[h] * outs[h], axis=-1, keepdims=True) + RMS_EPS) * nrm
        z = z_refs[h // heads_per_z][:, pl.ds((h % heads_per_z) * HEAD_DIM, HEAD_DIM)]
        o_ref[:, col(h)] = out * _silu(z)
        st_ref[h] = s_old[h] * jnp.exp(lasts[h]) + upds[h]

    @pl.when(ci == pl.num_programs(1) - 1)
    def _():
        sf_ref[...] = st_ref[...]


def _delta_rule(dq, proj, zcol0, gates, delta_norm, s0, n_b, t, c, d_delta, precise):
    n_h = d_delta // HEAD_DIM
    nc = t // c
    gr = gates[:, n_h:2 * n_h].reshape(n_b, nc, c, n_h).transpose(0, 1, 3, 2)
    zw = math.gcd(zcol0, d_delta)
    rows = lambda off: pl.BlockSpec((c, d_delta), lambda b, i, off=off: (b * nc + i, off))
    zspecs = [pl.BlockSpec((c, zw), lambda b, i, j=j: (b * nc + i, zcol0 // zw + j)) for j in range(d_delta // zw)]
    state = pl.BlockSpec((None, n_h, HEAD_DIM, HEAD_DIM), lambda b, i: (b, 0, 0, 0))
    return pl.pallas_call(
        functools.partial(_delta_kernel, c=c, n_h=n_h, n_z=len(zspecs), precise=precise),
        grid=(n_b, nc),
        in_specs=[rows(0), rows(1), rows(2), *zspecs,
                  pl.BlockSpec((c, LANES), lambda b, i: (b * nc + i, 0)),
                  pl.BlockSpec((None, None, n_h, c), lambda b, i: (b, i, 0, 0)),
                  pl.BlockSpec((1, HEAD_DIM), lambda b, i: (0, 0)),
                  state],
        out_specs=[rows(0), state],
        out_shape=[jax.ShapeDtypeStruct((n_b * t, d_delta), F32),
                   jax.ShapeDtypeStruct((n_b, n_h, HEAD_DIM, HEAD_DIM), F32)],
        scratch_shapes=[pltpu.VMEM((n_h, HEAD_DIM, HEAD_DIM), F32)],
        name="delta_rule",
        compiler_params=_cparams(("parallel", "arbitrary"), 32),
    )(dq, dq, dq, *([proj] * len(zspecs)), gates, gr, delta_norm.reshape(1, -1), s0)


def _router_kernel(h_ref, g_ref, wr_ref, xn_ref, meta_ref, cnt_ref, *, tm):
    @pl.when(pl.program_id(0) == 0)
    def _():
        cnt_ref[...] = jnp.zeros(cnt_ref.shape, F32)

    x = h_ref[...]
    xn = x * lax.rsqrt(jnp.mean(x * x, axis=-1, keepdims=True) + RMS_EPS) * g_ref[...]
    xn_ref[...] = xn
    logits = _dot3(xn, wr_ref[...])
    lane = lax.broadcasted_iota(I32, (tm, LANES), 1)
    big = jnp.int32(LANES)

    def first_max(vals, valid):
        vmax = jnp.max(jnp.where(valid, vals, -jnp.inf), axis=-1, keepdims=True)
        idx = jnp.min(jnp.where(valid & (vals == vmax), lane, big), axis=-1, keepdims=True)
        return vmax, idx

    gvalid = lane < N_GROUPS
    gmax, gstar = first_max(logits, gvalid)
    p_sel = 1.0 / jnp.sum(jnp.where(gvalid, jnp.exp(logits - gmax), 0.0), axis=-1, keepdims=True)
    e0 = N_GROUPS + gstar * EXPERTS_PER_GROUP
    evalid = (lane >= e0) & (lane < e0 + EXPERTS_PER_GROUP)
    emax, _ = first_max(logits, evalid)
    pe = jnp.where(evalid, jnp.exp(logits - emax), 0.0)
    pe = pe / jnp.sum(pe, axis=-1, keepdims=True)
    v1, i1 = first_max(pe, evalid)
    v2, i2 = first_max(pe, evalid & (lane != i1))
    den = v1 + v2
    g1 = p_sel * v1 / den
    g2 = p_sel * v2 / den
    ex1 = i1 - N_GROUPS
    ex2 = i2 - N_GROUPS
    oh1 = lane == ex1
    oh2 = lane == ex2
    oh = jnp.where(oh1 | oh2, 1.0, 0.0)
    ri = lax.broadcasted_iota(I32, (tm, tm), 0)
    rj = lax.broadcasted_iota(I32, (tm, tm), 1)
    before = _dot((ri > rj).astype(BF16), oh.astype(BF16)) + cnt_ref[...]
    r1 = jnp.sum(jnp.where(oh1, before, 0.0), axis=-1, keepdims=True)
    r2 = jnp.sum(jnp.where(oh2, before, 0.0), axis=-1, keepdims=True)
    cnt_ref[...] = cnt_ref[...] + jnp.sum(oh, axis=0, keepdims=True)
    meta = jnp.where(lane == 0, ex1.astype(F32),
           jnp.where(lane == 1, ex2.astype(F32),
           jnp.where(lane == 2, g1,
           jnp.where(lane == 3, g2,
           jnp.where(lane == 4, r1,
           jnp.where(lane == 5, r2, 0.0))))))
    meta_ref[...] = meta


def _router(h, gain, w_rg, w_re, tm):
    m, d = h.shape
    wr = jnp.zeros((d, LANES), F32)
    wr = wr.at[:, :N_GROUPS].set(w_rg)
    wr = wr.at[:, N_GROUPS:N_GROUPS + N_EXPERTS].set(w_re.transpose(1, 0, 2).reshape(d, N_EXPERTS))
    return pl.pallas_call(
        functools.partial(_router_kernel, tm=tm),
        grid=(m // tm,),
        in_specs=[pl.BlockSpec((tm, d), lambda i: (i, 0)),
                  pl.BlockSpec((1, d), lambda i: (0, 0)),
                  pl.BlockSpec((d, LANES), lambda i: (0, 0))],
        out_specs=[pl.BlockSpec((tm, d), lambda i: (i, 0)),
                   pl.BlockSpec((tm, LANES), lambda i: (i, 0)),
                   pl.BlockSpec((1, LANES), lambda i: (0, 0))],
        out_shape=[jax.ShapeDtypeStruct((m, d), F32),
                   jax.ShapeDtypeStruct((m, LANES), F32),
                   jax.ShapeDtypeStruct((1, LANES), F32)],
        name="router",
        compiler_params=_cparams(("arbitrary",), 56),
    )(h, gain.reshape(1, d), wr)


def _row_copy(src_hbm, dst, sem, src_row, dst_row):
    return pltpu.make_async_copy(src_hbm.at[pl.ds(src_row, 1), :], dst.at[pl.ds(dst_row, 1), :], sem)


def _gather_kernel(tok_ref, used_ref, x_hbm, o_ref, buf_ref, sem, *, rows):
    i = pl.program_id(0)
    slot = i % 2

    def issue(step, into):
        base = step * rows

        @pl.when(base < used_ref[0])
        def _():
            def body(r, c):
                _row_copy(x_hbm, buf_ref.at[into], sem.at[into], tok_ref[base + r], r).start()
                return c
            lax.fori_loop(0, rows, body, 0)

    @pl.when(i == 0)
    def _():
        issue(0, 0)

    @pl.when(i + 1 < pl.num_programs(0))
    def _():
        issue(i + 1, 1 - slot)

    @pl.when(i * rows < used_ref[0])
    def _():
        pltpu.make_async_copy(x_hbm.at[pl.ds(0, rows), :], buf_ref.at[slot], sem.at[slot]).wait()
        o_ref[...] = buf_ref[slot].astype(o_ref.dtype)


def _dispatch(xn, tok_of_row, used, n_rows, rows, out_dtype):
    d = xn.shape[1]
    return pl.pallas_call(
        functools.partial(_gather_kernel, rows=rows),
        grid_spec=pltpu.PrefetchScalarGridSpec(
            num_scalar_prefetch=2,
            grid=(n_rows // rows,),
            in_specs=[pl.BlockSpec(memory_space=pl.ANY)],
            out_specs=pl.BlockSpec((rows, d), lambda i, tok, used: (i, 0)),
            scratch_shapes=[pltpu.VMEM((2, rows, d), F32), pltpu.SemaphoreType.DMA((2,))]),
        out_shape=jax.ShapeDtypeStruct((n_rows, d), out_dtype),
        name="dispatch",
        compiler_params=_cparams(("arbitrary",), 32),
    )(tok_of_row, used, xn)


def _for_valid_rows(nrows, total, sub, fn):
    need = (nrows + sub - 1) // sub
    for q in range(1, total // sub + 1):
        @pl.when(need == q)
        def _(q=q):
            fn(q * sub)


def _expert_up_kernel(be_ref, br_ref, bs_ref, x_ref, wg_ref, wu_ref, h_ref, *, precise):
    w = pl.program_id(1)

    def run(rows):
        x = x_ref[pl.ds(0, rows), :]
        if precise:
            hg = _dot3(x, wg_ref[...])
            hu = _dot3(x, wu_ref[...])
        else:
            hg = _dot(x, wg_ref[...].astype(BF16))
            hu = _dot(x, wu_ref[...].astype(BF16))
        h_ref[pl.ds(0, rows), :] = (_silu(hg) * hu).astype(h_ref.dtype)

    _for_valid_rows(br_ref[w], x_ref.shape[0], MOE_SUB, run)


def _expert_down_kernel(be_ref, br_ref, bs_ref, h_ref, wd_ref, y_ref, *, precise):
    w = pl.program_id(1)

    def run(rows):
        hid = h_ref[pl.ds(0, rows), :]
        if precise:
            y_ref[pl.ds(0, rows), :] = _dot3(hid, wd_ref[...])
        else:
            y_ref[pl.ds(0, rows), :] = _dot(hid, wd_ref[...].astype(BF16))

    _for_valid_rows(br_ref[w], h_ref.shape[0], MOE_SUB, run)


def _experts(xs, blk_e, blk_rows, blk_src, w_gate, w_up, w_down, layer, r, precise):
    n_rows, d = xs.shape
    nb = n_rows // r
    d_e = w_gate.shape[-1]
    tc, tn = MOE_HID_TILE, MOE_OUT_TILE
    hid = pl.pallas_call(
        functools.partial(_expert_up_kernel, precise=precise),
        grid_spec=pltpu.PrefetchScalarGridSpec(
            num_scalar_prefetch=3,
            grid=(d_e // tc, nb),
            in_specs=[pl.BlockSpec((r, d), lambda c, w, be, br, bs: (bs[w], 0)),
                      pl.BlockSpec((None, None, d, tc), lambda c, w, be, br, bs: (layer, be[w], 0, c)),
                      pl.BlockSpec((None, None, d, tc), lambda c, w, be, br, bs: (layer, be[w], 0, c))],
            out_specs=pl.BlockSpec((r, tc), lambda c, w, be, br, bs: (bs[w], c))),
        out_shape=jax.ShapeDtypeStruct((n_rows, d_e), F32 if precise else BF16),
        name="experts_up",
        compiler_params=_cparams(("arbitrary", "arbitrary"), 48),
    )(blk_e, blk_rows, blk_src, xs, w_gate, w_up)
    return pl.pallas_call(
        functools.partial(_expert_down_kernel, precise=precise),
        grid_spec=pltpu.PrefetchScalarGridSpec(
            num_scalar_prefetch=3,
            grid=(d // tn, nb),
            in_specs=[pl.BlockSpec((r, d_e), lambda n, w, be, br, bs: (bs[w], 0)),
                      pl.BlockSpec((None, None, d_e, tn), lambda n, w, be, br, bs: (layer, be[w], 0, n))],
            out_specs=pl.BlockSpec((r, tn), lambda n, w, be, br, bs: (bs[w], n))),
        out_shape=jax.ShapeDtypeStruct((n_rows, d), F32),
        name="experts_down",
        compiler_params=_cparams(("arbitrary", "arbitrary"), 48),
    )(blk_e, blk_rows, blk_src, hid, w_down)


def _combine_kernel(dest_ref, y_hbm, h_ref, meta_ref, gf_ref, o_ref, buf_ref, sem, *, rows, final):
    base = pl.program_id(0) * rows

    def issue(r, c):
        t2 = 2 * (base + r)
        _row_copy(y_hbm, buf_ref.at[0], sem, dest_ref[t2], r).start()
        _row_copy(y_hbm, buf_ref.at[1], sem, dest_ref[t2 + 1], r).start()
        return c
    lax.fori_loop(0, rows, issue, 0)
    pltpu.make_async_copy(y_hbm.at[pl.ds(0, rows), :], buf_ref.at[0], sem).wait()
    pltpu.make_async_copy(y_hbm.at[pl.ds(0, rows), :], buf_ref.at[1], sem).wait()
    meta = meta_ref[...]
    out = h_ref[...] + (buf_ref[0] * meta[:, 2:3] + buf_ref[1] * meta[:, 3:4])
    if final:
        out = out * lax.rsqrt(jnp.mean(out * out, axis=-1, keepdims=True) + RMS_EPS) * gf_ref[...]
    o_ref[...] = out


def _combine(y, dest, h, meta, norm_final, rows, final):
    m, d = h.shape
    return pl.pallas_call(
        functools.partial(_combine_kernel, rows=rows, final=final),
        grid_spec=pltpu.PrefetchScalarGridSpec(
            num_scalar_prefetch=1,
            grid=(m // rows,),
            in_specs=[pl.BlockSpec(memory_space=pl.ANY),
                      pl.BlockSpec((rows, d), lambda i, dest: (i, 0)),
                      pl.BlockSpec((rows, LANES), lambda i, dest: (i, 0)),
                      pl.BlockSpec((1, d), lambda i, dest: (0, 0))],
            out_specs=pl.BlockSpec((rows, d), lambda i, dest: (i, 0)),
            scratch_shapes=[pltpu.VMEM((2, rows, d), F32), pltpu.SemaphoreType.DMA(())]),
        out_shape=jax.ShapeDtypeStruct((m, d), F32),
        name="combine",
        compiler_params=_cparams(("arbitrary",), 48),
    )(dest, y, h, meta, norm_final.reshape(1, d))


def _moe(h, norm_ffn, w_rg, w_re, w_gate, w_up, w_down, norm_final, *, layer, final, precise, r, token_rows,
         gather_rows):
    m, d = h.shape
    xn, meta, counts = _router(h, norm_ffn, w_rg, w_re, token_rows)
    experts = meta[:, 0:2].astype(I32)
    rank = meta[:, 4:6].astype(I32)
    cnt = counts[0, :N_EXPERTS].astype(I32)
    padded = (cnt + r - 1) // r * r
    pad_end = jnp.cumsum(padded)
    pad_start = pad_end - padded
    dest = pad_start[experts] + rank
    nb = -(-(m * TOP_K) // r) + N_EXPERTS
    blk_start = jnp.arange(nb, dtype=I32) * r
    used = pad_end[-1]
    active = blk_start < used
    blk_e = jnp.minimum(jnp.sum(pad_end[None, :] <= blk_start[:, None], axis=1), N_EXPERTS - 1).astype(I32)
    blk_rows = jnp.where(active, jnp.clip(cnt[blk_e] - (blk_start - pad_start[blk_e]), 0, r), 0).astype(I32)
    last_blk = jnp.maximum(used // r - 1, 0)
    blk_e = jnp.where(active, blk_e, blk_e[last_blk])
    blk_src = jnp.minimum(jnp.arange(nb, dtype=I32), last_blk).astype(I32)
    tok_of_row = jnp.zeros((nb * r,), I32).at[dest.reshape(-1)].set(
        jnp.repeat(jnp.arange(m, dtype=I32), TOP_K))
    xs = _dispatch(xn, tok_of_row, used.reshape(1).astype(I32), nb * r, gather_rows, F32 if precise else BF16)
    y = _experts(xs, blk_e, blk_rows, blk_src, w_gate, w_up, w_down, layer, r, precise)
    return _combine(y, dest.reshape(-1).astype(I32), h, meta, norm_final, token_rows, final)


def kernel(x_prompt, x_sample, cache_k, cache_v, state_pool, state_conv, state_delta, page_table, meta, norm_mix, w_in, lambda_q1, lambda_k1, lambda_q2, lambda_k2, subln_gain, w_pool, pool_scale, conv_w, a_log, dt_bias, delta_norm, w_out, norm_ffn, w_router_group, w_router_expert, w_gate, w_up, w_down, norm_final):
    depth = w_in.shape[0]
    n_b, seq, d_model = x_prompt.shape
    n_s, ts, _ = x_sample.shape
    tp = N_META + seq
    tpad = SUBLANES
    n_mix = d_model // HEAD_DIM
    d_attn = (3 * n_mix) // 8 * HEAD_DIM
    d_delta = d_attn
    d_pool = d_model - d_attn - d_delta
    n_dh = d_delta // HEAD_DIM
    d_in = w_in.shape[-1]
    col_u = 3 * d_attn
    col_c = col_u + d_pool
    col_z = col_c + 3 * d_delta
    col_ba = col_z + d_delta
    d_in_pad = -(-d_in // PROJ_COL_TILE) * PROJ_COL_TILE
    mp = n_b * tp
    past = page_table.shape[1] * PAGE_SIZE
    slopes = _alibi_slopes(d_attn // HEAD_DIM)

    cache_kh = cache_k.transpose(0, 1, 3, 2, 4)
    cache_vh = cache_v.transpose(0, 1, 3, 2, 4)
    w_in_t = w_in.transpose(0, 2, 1)
    w_in_tb = w_in_t.astype(BF16)
    w_out_b = w_out.astype(BF16)

    hp = jnp.concatenate([jnp.broadcast_to(meta[None], (n_b, N_META, d_model)), x_prompt], axis=1)
    xp = hp.reshape(mp, d_model)
    xs = x_sample.reshape(n_s * ts, d_model)

    def pad_rows(a):
        a = a.reshape(n_s, ts, a.shape[-1])
        return jnp.pad(a, ((0, 0), (0, tpad - ts), (0, 0))).reshape(n_s * tpad, a.shape[-1])

    def real_rows(a):
        return a.reshape(n_s, tpad, -1)[:, :ts].reshape(n_s * ts, -1)

    def mixers(proj, oa, pool_prefix, conv_prefix, s0, n_seq, t, chunk, pos0, live, precise):
        ob = _pool_mix(proj, col_u, pool_prefix, w_pool[l], pool_scale[l], n_seq, t, pos0, precise)
        dq = _delta_prep(proj, col_c, conv_prefix, conv_w[l], n_seq, t, d_delta)
        gates = _gates(proj, col_ba, a_log[l], dt_bias[l], n_seq * t, t, n_dh)
        if live is not None:
            gates = jnp.where(live, gates, 0.0)
        oc, sf = _delta_rule(dq, proj, col_z, gates, delta_norm[l], s0, n_seq, t, chunk, d_delta, precise)
        return jnp.concatenate([oa, ob, oc], axis=1), sf

    moe_p = dict(precise=False, r=MOE_ROWS, token_rows=PROMPT_TOKEN_ROWS, gather_rows=GATHER_ROWS)
    moe_s = dict(precise=True, r=MOE_SUB, token_rows=n_s * ts, gather_rows=MOE_SUB)

    outs = {k: [] for k in ('kp', 'vp', 'pp', 'cp', 'dp', 'ks', 'vs', 'ps', 'cs', 'ds')}
    for l in range(depth):
        lw = {'lambda_q1': lambda_q1[l], 'lambda_k1': lambda_k1[l], 'lambda_q2': lambda_q2[l],
              'lambda_k2': lambda_k2[l], 'subln_gain': subln_gain[l]}
        lam_init = 0.8 - 0.6 * math.exp(-0.3 * l)
        final = l == depth - 1
        moe_w = (norm_ffn[l], w_router_group[l], w_router_expert[l], w_gate, w_up, w_down, norm_final)

        proj = _dense(xp, w_in_tb, gain=norm_mix[l], tm=PROMPT_ROW_TILE, layer=l, w_transposed=True)
        oa_p, k_p, v_p = _prompt_attention(proj, lw, lam_init, jnp.asarray(slopes), n_b, tp, d_attn)
        act, sf_p = mixers(
            proj, oa_p, jnp.zeros((n_b, 16, d_pool), F32), jnp.zeros((n_b, 8, 3 * d_delta), F32),
            jnp.zeros((n_b, n_dh, HEAD_DIM, HEAD_DIM), F32), n_b, tp, DELTA_CHUNK, 0, None, False)
        h = _dense(act, w_out_b, res=xp, tm=PROMPT_ROW_TILE, layer=l)
        if not final:
            xp = _moe(h, *moe_w, layer=l, final=False, **moe_p)

        proj_s32 = _dense(xs, w_in_t, gain=norm_mix[l], tm=n_s * ts, precise=True, layer=l,
                          w_transposed=True)
        proj_s3 = proj_s32.reshape(n_s, ts, d_in_pad)
        proj_s = pad_rows(proj_s32)
        pre_s = jnp.concatenate([jnp.zeros((n_s, 16 - POOL_BUF, d_pool), F32), state_pool[l]], axis=1)
        cpre_s = jnp.concatenate([jnp.zeros((n_s, 8 - (CONV_W - 1), 3 * d_delta), F32), state_conv[l]], axis=1)
        live = (jnp.arange(n_s * tpad) % tpad < ts)[:, None]
        oa_s = _paged_attention(proj_s, cache_kh, cache_vh, l, page_table, lw, lam_init, slopes, ts, tpad, d_attn)
        act_s, sf_s = mixers(proj_s, oa_s, pre_s, cpre_s, state_delta[l], n_s, tpad, tpad, past, live, True)
        h_s = _dense(real_rows(act_s), w_out, res=xs, tm=n_s * ts, precise=True, layer=l)
        if not final:
            xs = _moe(h_s, *moe_w, layer=l, final=False, **moe_s)
        else:
            both = _moe(jnp.concatenate([h, h_s], axis=0), *moe_w, layer=l, final=True,
                        **dict(moe_p, token_rows=ALL_TOKEN_ROWS))
            xp, xs = both[:mp], both[mp:]

        pp3 = proj.reshape(n_b, tp, d_in_pad)
        n_ah = d_attn // HEAD_DIM
        outs['kp'].append(k_p.transpose(0, 2, 1, 3))
        outs['vp'].append(v_p.transpose(0, 2, 1, 3))
        outs['pp'].append(pp3[:, tp - POOL_BUF:, col_u:col_u + d_pool])
        outs['cp'].append(pp3[:, tp - (CONV_W - 1):, col_c:col_c + 3 * d_delta])
        outs['dp'].append(sf_p)
        outs['ks'].append(proj_s3[:, :, d_attn:2 * d_attn].reshape(n_s, ts, n_ah, HEAD_DIM))
        outs['vs'].append(proj_s3[:, :, 2 * d_attn:3 * d_attn].reshape(n_s, ts, n_ah, HEAD_DIM))
        u_s = proj_s3[:, :, col_u:col_u + d_pool]
        outs['ps'].append(jnp.concatenate([state_pool[l], u_s], axis=1)[:, -POOL_BUF:])
        c_s = proj_s3[:, :, col_c:col_c + 3 * d_delta]
        outs['cs'].append(jnp.concatenate([state_conv[l], c_s], axis=1)[:, -(CONV_W - 1):])
        outs['ds'].append(sf_s)

    y_prompt = xp.reshape(n_b, tp, d_model)[:, N_META:]
    y_sample = xs.reshape(n_s, ts, d_model)
    st = lambda k: jnp.stack(outs[k])
    return (y_prompt, y_sample, st('kp'), st('vp'), st('pp'), st('cp'), st('dp'),
            st('ks'), st('vs'), st('ps'), st('cs'), st('ds'))
```

```python
import functools
import math

import numpy as np
import jax
import jax.numpy as jnp
from jax import lax
from jax.experimental import pallas as pl
from jax.experimental.pallas import tpu as pltpu

F32 = jnp.float32
BF16 = jnp.bfloat16
I32 = jnp.int32

HEAD_DIM = 128
QK_HALF = HEAD_DIM // 2
POOL_WINDOWS = (2, 4, 8, 16)
POOL_BUF = 15
CONV_W = 4
N_META = 16
N_GROUPS = 4
EXPERTS_PER_GROUP = 8
N_EXPERTS = N_GROUPS * EXPERTS_PER_GROUP
TOP_K = 2
PAGE_SIZE = 128
RMS_EPS = 1e-6
NEG_INF = -1e30

LANES = 128
SUBLANES = 8
MIB = 1024 * 1024

PROMPT_ROW_TILE = 688
PROMPT_TOKEN_ROWS = 344
ALL_TOKEN_ROWS = 296
PROJ_COL_TILE = 512
ATTN_BLOCK = 256
DELTA_CHUNK = 48
DELTA_BASE = 8
PAGES_PER_STEP = 4
PAGED_HEAD_GROUP = 12
MOE_ROWS = 512
MOE_SUB = 128
MOE_HID_TILE = 256
MOE_OUT_TILE = 1024
GATHER_ROWS = 256


def _cparams(semantics, vmem_mib):
    return pltpu.CompilerParams(dimension_semantics=semantics, vmem_limit_bytes=vmem_mib * MIB)


def _alibi_slopes(n_heads):
    def pow2_slopes(m):
        start = 2.0 ** (-8.0 / m)
        return [start ** (i + 1) for i in range(m)]
    closest = 2 ** int(math.floor(math.log2(n_heads)))
    slopes = pow2_slopes(closest) + pow2_slopes(2 * closest)[0::2][: n_heads - closest]
    return np.array(slopes, dtype=np.float32)


def _sigmoid(x):
    return 1.0 / (1.0 + jnp.exp(-x))


def _silu(x):
    return x * _sigmoid(x)


def _split_bf16(x):
    hi = x.astype(BF16)
    lo = (x - hi.astype(F32)).astype(BF16)
    return hi, lo


def _stack_hilo(x):
    hi = x.astype(BF16).astype(F32)
    return jnp.concatenate([hi, x - hi], axis=0).astype(BF16)


def _dot(a, b):
    return jnp.dot(a, b, preferred_element_type=F32)


def _dot_nt(a, b):
    return lax.dot_general(a, b, (((1,), (1,)), ((), ())), preferred_element_type=F32)


def _dot_tn(a, b):
    return lax.dot_general(a, b, (((0,), (0,)), ((), ())), preferred_element_type=F32)


def _dot3(a, b, dot=_dot):
    ah, al = _split_bf16(a)
    bh, bl = _split_bf16(b)
    return dot(ah, bh) + (dot(ah, bl) + dot(al, bh))


def _proj_kernel(*refs, normed, residual, precise, w_transposed):
    mm = _dot_nt if w_transposed else _dot
    it = iter(refs)
    x_ref = next(it)
    g_ref = next(it) if normed else None
    w_ref = next(it)
    r_ref = next(it) if residual else None
    o_ref = next(it)
    xh_ref = next(it)
    xl_ref = next(it) if precise else None

    @pl.when(pl.program_id(1) == 0)
    def _():
        x = x_ref[...]
        if normed:
            ms = jnp.mean(x * x, axis=-1, keepdims=True)
            x = x * lax.rsqrt(ms + RMS_EPS) * g_ref[...]
        if precise:
            xh_ref[...], xl_ref[...] = _split_bf16(x)
        else:
            xh_ref[...] = x.astype(BF16)

    if precise:
        wh, wl = _split_bf16(w_ref[...])
        acc = mm(xh_ref[...], wh) + (mm(xh_ref[...], wl) + mm(xl_ref[...], wh))
    else:
        acc = mm(xh_ref[...], w_ref[...])
    if residual:
        acc = r_ref[...] + acc
    o_ref[...] = acc


def _dense(x, w, gain=None, res=None, *, tm, tn=PROJ_COL_TILE, precise=False, layer=None, w_transposed=False):
    m, d = x.shape
    n = w.shape[-2] if w_transposed else w.shape[-1]
    n_out = pl.cdiv(n, tn) * tn
    assert m % tm == 0
    in_specs = [pl.BlockSpec((tm, d), lambda i, j: (i, 0))]
    args = [x]
    if gain is not None:
        in_specs.append(pl.BlockSpec((1, d), lambda i, j: (0, 0)))
        args.append(gain.reshape(1, d))
    if layer is None:
        in_specs.append(pl.BlockSpec((d, tn), lambda i, j: (0, j)))
    elif w_transposed:
        in_specs.append(pl.BlockSpec((None, tn, d), lambda i, j: (layer, j, 0)))
    else:
        in_specs.append(pl.BlockSpec((None, d, tn), lambda i, j: (layer, 0, j)))
    args.append(w)
    if res is not None:
        in_specs.append(pl.BlockSpec((tm, tn), lambda i, j: (i, j)))
        args.append(res)
    return pl.pallas_call(
        functools.partial(_proj_kernel, normed=gain is not None, residual=res is not None, precise=precise,
                          w_transposed=w_transposed),
        grid=(m // tm, n_out // tn),
        in_specs=in_specs,
        out_specs=pl.BlockSpec((tm, tn), lambda i, j: (i, j)),
        out_shape=jax.ShapeDtypeStruct((m, n_out), F32),
        scratch_shapes=[pltpu.VMEM((tm, d), BF16)] * (2 if precise else 1),
        name="dense_precise" if precise else "dense",
        compiler_params=_cparams(("parallel", "arbitrary"), 56),
    )(*args)


def _lambda_value(lq1, lk1, lq2, lk2, lam_init):
    a = jnp.sum(lq1[...] * lk1[...], axis=-1, keepdims=True)
    b = jnp.sum(lq2[...] * lk2[...], axis=-1, keepdims=True)
    return jnp.exp(a) - jnp.exp(b) + lam_init


def _subln(o, gain, lam_init):
    o = o * lax.rsqrt(jnp.mean(o * o, axis=-1, keepdims=True) + RMS_EPS) * gain
    return o * (1.0 - lam_init)


def _prompt_attn_kernel(slopes_ref, lq1, lk1, lq2, lk2, gain_ref, q_ref, k_ref, v_ref, o_ref, ko_ref, vo_ref,
                        qp_ref, kp_ref, vp_ref, op_ref, *, t, tb, lam_init):
    h = pl.program_id(1)
    ko_ref[...] = k_ref[...]
    vo_ref[...] = v_ref[...]
    slope = slopes_ref[h]
    lam = _lambda_value(lq1, lk1, lq2, lk2, lam_init)
    tp = qp_ref.shape[0]
    nb = tp // tb
    for src, dst in ((q_ref, qp_ref), (k_ref, kp_ref), (v_ref, vp_ref)):
        dst[pl.ds(0, t), :] = src[...]
        dst[pl.ds(t, tp - t), :] = jnp.zeros((tp - t, HEAD_DIM), F32)
    dloc = lax.broadcasted_iota(I32, (tb, tb), 0) - lax.broadcasted_iota(I32, (tb, tb), 1)

    def qblock(qi, carry):
        r0 = pl.multiple_of(qi * tb, tb)
        q = qp_ref[pl.ds(r0, tb), :] * (QK_HALF ** -0.5)
        q1 = q[:, :QK_HALF].astype(BF16)
        q2 = q[:, QK_HALF:].astype(BF16)

        def kvblock(kj, c):
            m1, l1, a1, m2, l2, a2 = c
            c0 = pl.multiple_of(kj * tb, tb)
            kk = kp_ref[pl.ds(c0, tb), :]
            vv = vp_ref[pl.ds(c0, tb), :].astype(BF16)
            dist = dloc + (r0 - c0)
            causal = dist >= 0
            bias = slope * dist.astype(F32)

            s1 = _dot_nt(q1, kk[:, :QK_HALF].astype(BF16))
            s2 = _dot_nt(q2, kk[:, QK_HALF:].astype(BF16))
            s1 = jnp.where(causal, s1 - bias, NEG_INF)
            s2 = jnp.where(causal, s2 - bias, NEG_INF)
            n1 = jnp.maximum(m1, jnp.max(s1, axis=-1, keepdims=True))
            n2 = jnp.maximum(m2, jnp.max(s2, axis=-1, keepdims=True))
            p1 = jnp.exp(s1 - n1)
            p2 = jnp.exp(s2 - n2)
            al1 = jnp.exp(m1 - n1)
            al2 = jnp.exp(m2 - n2)
            pv1 = _dot(p1.astype(BF16), vv)
            pv2 = _dot(p2.astype(BF16), vv)
            l1 = al1 * l1 + jnp.sum(p1, axis=-1, keepdims=True)
            l2 = al2 * l2 + jnp.sum(p2, axis=-1, keepdims=True)
            return n1, l1, al1 * a1 + pv1, n2, l2, al2 * a2 + pv2

        m0 = jnp.full((tb, 1), NEG_INF, F32)
        z1 = jnp.zeros((tb, 1), F32)
        za = jnp.zeros((tb, HEAD_DIM), F32)
        m1, l1, a1, m2, l2, a2 = lax.fori_loop(0, qi + 1, kvblock, (m0, z1, za, m0, z1, za))
        o = a1 / l1 - lam * (a2 / l2)
        op_ref[pl.ds(r0, tb), :] = _subln(o, gain_ref[...], lam_init)
        return carry

    lax.fori_loop(0, nb, qblock, 0)
    o_ref[...] = op_ref[pl.ds(0, t), :]


def _prompt_attention(proj, lw, lam_init, slopes, n_b, t, d_attn):
    n_h = d_attn // HEAD_DIM
    tb = ATTN_BLOCK
    tp = -(-t // tb) * tb
    vec = lambda n: pl.BlockSpec((1, n), lambda b, h: (0, 0))
    blk = lambda off: pl.BlockSpec((t, HEAD_DIM), lambda b, h, off=off: (b, off + h))
    return pl.pallas_call(
        functools.partial(_prompt_attn_kernel, t=t, tb=tb, lam_init=lam_init),
        grid=(n_b, n_h),
        in_specs=[pl.BlockSpec(memory_space=pltpu.SMEM),
                  vec(QK_HALF), vec(QK_HALF), vec(QK_HALF), vec(QK_HALF), vec(HEAD_DIM),
                  blk(0), blk(n_h), blk(2 * n_h)],
        out_specs=[pl.BlockSpec((t, HEAD_DIM), lambda b, h: (b, h)),
                   pl.BlockSpec((None, None, t, HEAD_DIM), lambda b, h: (b, h, 0, 0)),
                   pl.BlockSpec((None, None, t, HEAD_DIM), lambda b, h: (b, h, 0, 0))],
        out_shape=[jax.ShapeDtypeStruct((n_b * t, d_attn), F32),
                   jax.ShapeDtypeStruct((n_b, n_h, t, HEAD_DIM), F32),
                   jax.ShapeDtypeStruct((n_b, n_h, t, HEAD_DIM), F32)],
        scratch_shapes=[pltpu.VMEM((tp, HEAD_DIM), F32)] * 4,
        name="prompt_attention",
        compiler_params=_cparams(("parallel", "parallel"), 40),
    )(slopes, lw['lambda_q1'].reshape(1, -1), lw['lambda_k1'].reshape(1, -1),
      lw['lambda_q2'].reshape(1, -1), lw['lambda_k2'].reshape(1, -1), lw['subln_gain'].reshape(1, -1),
      proj, proj, proj)


def _paged_attn_kernel(pt_ref, lq1, lk1, lq2, lk2, gain_ref, qkv_ref, *rest, ts, n_h, n_pages, pps, lam_init,
                       slopes):
    k_refs, v_refs = rest[:pps], rest[pps:2 * pps]
    o_ref, qm_ref, m_ref, l_ref, acc_ref = rest[2 * pps:]
    p = pl.program_id(1)
    d_attn = n_h * HEAD_DIM
    past = n_pages * PAGE_SIZE
    width = pps * PAGE_SIZE
    rows2 = 2 * ts

    @pl.when(p == 0)
    def _():
        lane = lax.broadcasted_iota(I32, (rows2, HEAD_DIM), 1)
        srow = lax.broadcasted_iota(I32, (rows2, HEAD_DIM), 0)
        for h in range(n_h):
            q = qkv_ref[:, pl.ds(h * HEAD_DIM, HEAD_DIM)] * (QK_HALF ** -0.5)
            qa = jnp.where(lane < QK_HALF, q, 0.0)
            qb = jnp.where(lane >= QK_HALF, pltpu.roll(q, ts, 0), 0.0)
            qm_ref[h] = jnp.where(srow < ts, qa, qb)
        m_ref[...] = jnp.full(m_ref.shape, NEG_INF, F32)
        l_ref[...] = jnp.zeros(l_ref.shape, F32)
        acc_ref[...] = jnp.zeros(acc_ref.shape, F32)

    def stacked3(lhs, rhs, dot):
        rh, rl = _split_bf16(rhs)
        both = dot(_stack_hilo(lhs), rh)
        return both[:rows2] + (both[rows2:] + dot(lhs.astype(BF16), rl))

    def softmax_step(h, s, v, m_old, l_old, acc_old):
        mn = jnp.maximum(m_old, jnp.max(s, axis=-1, keepdims=True))
        pr = jnp.exp(s - mn)
        al = jnp.exp(m_old - mn)
        return mn, al * l_old + jnp.sum(pr, axis=-1, keepdims=True), al * acc_old + stacked3(pr, v, _dot)

    row = lax.broadcasted_iota(I32, (rows2, width), 0)
    col = lax.broadcasted_iota(I32, (rows2, width), 1)
    tok = jnp.where(row >= ts, row - ts, row)
    dist = (past - p * width) + tok - col
    distf = dist.astype(F32)
    cat = lambda refs, h: jnp.concatenate([refs[i][h] for i in range(pps)], axis=0)
    for h0 in range(0, n_h, PAGED_HEAD_GROUP):
        heads = range(h0, min(h0 + PAGED_HEAD_GROUP, n_h))
        ksplit = {h: _split_bf16(cat(k_refs, h)) for h in heads}
        qstack = {h: _stack_hilo(qm_ref[h]) for h in heads}
        both = {h: _dot_nt(qstack[h], ksplit[h][0]) for h in heads}
        cross = {h: _dot_nt(qm_ref[h].astype(BF16), ksplit[h][1]) for h in heads}
        ss = {h: both[h][:rows2] + (both[h][rows2:] + cross[h]) - float(slopes[h]) * distf for h in heads}
        m_old = {h: m_ref[h] for h in heads}
        m_new = {h: jnp.maximum(m_old[h], jnp.max(ss[h], axis=-1, keepdims=True)) for h in heads}
        prs = {h: jnp.exp(ss[h] - m_new[h]) for h in heads}
        als = {h: jnp.exp(m_old[h] - m_new[h]) for h in heads}
        vsplit = {h: _split_bf16(cat(v_refs, h)) for h in heads}
        pstack = {h: _stack_hilo(prs[h]) for h in heads}
        pboth = {h: _dot(pstack[h], vsplit[h][0]) for h in heads}
        pcross = {h: _dot(prs[h].astype(BF16), vsplit[h][1]) for h in heads}
        for h in heads:
            l_ref[h] = als[h] * l_ref[h] + jnp.sum(prs[h], axis=-1, keepdims=True)
            acc_ref[h] = als[h] * acc_ref[h] + (pboth[h][:rows2] + (pboth[h][rows2:] + pcross[h]))
            m_ref[h] = m_new[h]

    def update(h, s, v):
        m_ref[h], l_ref[h], acc_ref[h] = softmax_step(h, s, v, m_ref[h], l_ref[h], acc_ref[h])

    @pl.when(p == n_pages // pps - 1)
    def _():
        lam = _lambda_value(lq1, lk1, lq2, lk2, lam_init)
        rown = lax.broadcasted_iota(I32, (rows2, rows2), 0)
        coln = lax.broadcasted_iota(I32, (rows2, rows2), 1)
        tokn = jnp.where(rown >= ts, rown - ts, rown)
        distn = tokn - coln
        valid = (distn >= 0) & (coln < ts)
        for h in range(n_h):
            kn = qkv_ref[:, pl.ds(d_attn + h * HEAD_DIM, HEAD_DIM)]
            vn = qkv_ref[:, pl.ds(2 * d_attn + h * HEAD_DIM, HEAD_DIM)]
            s = stacked3(qm_ref[h], kn, _dot_nt)
            s = jnp.where(valid, s - float(slopes[h]) * distn.astype(F32), NEG_INF)
            update(h, s, vn)
            o = acc_ref[h] / l_ref[h]
            o = o - lam * pltpu.roll(o, ts, 0)
            o_ref[:, pl.ds(h * HEAD_DIM, HEAD_DIM)] = _subln(o, gain_ref[...], lam_init)


def _paged_attention(proj_s, cache_k, cache_v, layer, page_table, lw, lam_init, slopes, ts, tpad, d_attn):
    n_seq, n_pages = page_table.shape
    n_h = d_attn // HEAD_DIM
    pps = PAGES_PER_STEP
    assert n_pages % pps == 0
    vec = lambda n: pl.BlockSpec((1, n), lambda s, p, pt: (0, 0))
    cache_specs = [pl.BlockSpec((None, None, n_h, PAGE_SIZE, HEAD_DIM),
                                lambda s, p, pt, i=i: (layer, pt[s * n_pages + p * pps + i], 0, 0, 0))
                   for i in range(pps)]
    return pl.pallas_call(
        functools.partial(_paged_attn_kernel, ts=ts, n_h=n_h, n_pages=n_pages, pps=pps, lam_init=lam_init,
                          slopes=tuple(float(x) for x in slopes)),
        grid_spec=pltpu.PrefetchScalarGridSpec(
            num_scalar_prefetch=1,
            grid=(n_seq, n_pages // pps),
            in_specs=[vec(QK_HALF), vec(QK_HALF), vec(QK_HALF), vec(QK_HALF), vec(HEAD_DIM),
                      pl.BlockSpec((tpad, 3 * d_attn), lambda s, p, pt: (s, 0)),
                      *cache_specs, *cache_specs],
            out_specs=pl.BlockSpec((tpad, d_attn), lambda s, p, pt: (s, 0)),
            scratch_shapes=[pltpu.VMEM((n_h, 2 * ts, HEAD_DIM), F32),
                            pltpu.VMEM((n_h, 2 * ts, 1), F32),
                            pltpu.VMEM((n_h, 2 * ts, 1), F32),
                            pltpu.VMEM((n_h, 2 * ts, HEAD_DIM), F32)]),
        out_shape=jax.ShapeDtypeStruct((n_seq * tpad, d_attn), F32),
        name="paged_attention",
        compiler_params=_cparams(("parallel", "arbitrary"), 48),
    )(page_table.reshape(-1), lw['lambda_q1'].reshape(1, -1), lw['lambda_k1'].reshape(1, -1),
      lw['lambda_q2'].reshape(1, -1), lw['lambda_k2'].reshape(1, -1), lw['subln_gain'].reshape(1, -1),
      proj_s, *([cache_k] * pps), *([cache_v] * pps))


def _pool_kernel(u_ref, pre_ref, w_ref, sc_ref, o_ref, buf_ref, *, t, pos0, precise):
    g = pl.program_id(1)
    hist = pre_ref.shape[0]
    buf_ref[pl.ds(0, hist), :] = pre_ref[...]
    buf_ref[pl.ds(hist, t), :] = u_ref[...]
    pos = pos0 + 1 + lax.broadcasted_iota(I32, (t, 1), 0)
    for gi, w in enumerate(POOL_WINDOWS):
        @pl.when(g == gi)
        def _(w=w):
            acc = buf_ref[pl.ds(hist, t), :]
            for i in range(1, w):
                acc = acc + buf_ref[pl.ds(hist - i, t), :]
            cnt = jnp.minimum(pos, w).astype(F32)
            diff = acc / cnt - u_ref[...]
            if precise:
                mixed = _dot3(diff, w_ref[...])
            else:
                mixed = _dot(diff.astype(BF16), w_ref[...].astype(BF16))
            o_ref[...] = mixed * sc_ref[...]


def _pool_mix(proj, col0, prefix16, w_pool, scale, n_b, t, pos0, precise):
    n_g = len(POOL_WINDOWS)
    grp = w_pool.shape[-1]
    cb = col0 // grp
    return pl.pallas_call(
        functools.partial(_pool_kernel, t=t, pos0=pos0, precise=precise),
        grid=(n_b, n_g),
        in_specs=[pl.BlockSpec((t, grp), lambda b, g: (b, cb + g)),
                  pl.BlockSpec((None, 16, grp), lambda b, g: (b, 0, g)),
                  pl.BlockSpec((None, grp, grp), lambda b, g: (g, 0, 0)),
                  pl.BlockSpec((1, grp), lambda b, g: (0, g))],
        out_specs=pl.BlockSpec((t, grp), lambda b, g: (b, g)),
        out_shape=jax.ShapeDtypeStruct((n_b * t, n_g * grp), F32),
        scratch_shapes=[pltpu.VMEM((16 + t, grp), F32)],
        name="pool_mix",
        compiler_params=_cparams(("parallel", "parallel"), 32),
    )(proj, prefix16, w_pool, scale.reshape(1, -1))


def _dprep_kernel(x_ref, pre_ref, cw_ref, o_ref, buf_ref, *, t, n_h):
    j = pl.program_id(1)
    buf_ref[pl.ds(0, 8), :] = pre_ref[...]
    buf_ref[pl.ds(8, t), :] = x_ref[...]
    acc = buf_ref[pl.ds(8 - (CONV_W - 1), t), :] * cw_ref[pl.ds(0, 1), :]
    for i in range(1, CONV_W):
        acc = acc + buf_ref[pl.ds(8 - (CONV_W - 1) + i, t), :] * cw_ref[pl.ds(i, 1), :]
    s = _silu(acc)
    nrm = s * lax.rsqrt(jnp.sum(s * s, axis=-1, keepdims=True) + RMS_EPS)
    o_ref[...] = jnp.where(j < n_h, nrm * (HEAD_DIM ** -0.5), jnp.where(j < 2 * n_h, nrm, s))


def _delta_prep(proj, col0, prefix8, conv_w, n_b, t, d_delta):
    n_h = d_delta // HEAD_DIM
    cb = col0 // HEAD_DIM
    return pl.pallas_call(
        functools.partial(_dprep_kernel, t=t, n_h=n_h),
        grid=(n_b, 3 * n_h),
        in_specs=[pl.BlockSpec((t, HEAD_DIM), lambda b, j: (b, cb + j)),
                  pl.BlockSpec((None, 8, HEAD_DIM), lambda b, j: (b, 0, j)),
                  pl.BlockSpec((CONV_W, HEAD_DIM), lambda b, j: (0, j))],
        out_specs=pl.BlockSpec((t, HEAD_DIM), lambda b, j: (b, j)),
        out_shape=jax.ShapeDtypeStruct((n_b * t, 3 * d_delta), F32),
        scratch_shapes=[pltpu.VMEM((8 + t, HEAD_DIM), F32)],
        name="delta_prep",
        compiler_params=_cparams(("parallel", "parallel"), 32),
    )(proj, prefix8, conv_w)


def _gates_kernel(x_ref, alog_ref, dtb_ref, o_ref, *, n_h):
    x = x_ref[...]
    lane = lax.broadcasted_iota(I32, x.shape, 1)
    beta = _sigmoid(x)
    z = x + dtb_ref[...]
    softplus = jnp.maximum(z, 0.0) + jnp.log(1.0 + jnp.exp(-jnp.abs(z)))
    g = -jnp.exp(alog_ref[...]) * softplus
    o_ref[...] = jnp.where(lane < n_h, beta, g)


def _gates(proj, col0, a_log, dt_bias, rows, tm, n_h):
    cb = col0 // LANES
    pad = lambda v: jnp.zeros((1, LANES), F32).at[0, n_h:2 * n_h].set(v.astype(F32))
    return pl.pallas_call(
        functools.partial(_gates_kernel, n_h=n_h),
        grid=(rows // tm,),
        in_specs=[pl.BlockSpec((tm, LANES), lambda i: (i, cb)),
                  pl.BlockSpec((1, LANES), lambda i: (0, 0)),
                  pl.BlockSpec((1, LANES), lambda i: (0, 0))],
        out_specs=pl.BlockSpec((tm, LANES), lambda i: (i, 0)),
        out_shape=jax.ShapeDtypeStruct((rows, LANES), F32),
        name="delta_gates",
        compiler_params=_cparams(("parallel",), 32),
    )(proj, pad(a_log), pad(dt_bias))


def _unit_lower_inverses(lms, c, mm):
    ii = lax.broadcasted_iota(I32, (c, c), 0)
    jj = lax.broadcasted_iota(I32, (c, c), 1)
    eye = (ii == jj).astype(F32)
    base = DELTA_BASE
    shift = int(math.log2(base))
    same = (ii >> shift) == (jj >> shift)
    npows = [-jnp.where(same, lm, 0.0) for lm in lms]
    invs = [eye + n0 for n0 in npows]
    for _ in range(shift - 1):
        npows = [mm(n, n) for n in npows]
        invs = [inv + mm(inv, n) for inv, n in zip(invs, npows)]
    size = base
    while size < c:
        s = int(math.log2(size))
        sib = ((ii >> (s + 1)) == (jj >> (s + 1))) & ((ii >> s) != (jj >> s))
        tmps = [mm(jnp.where(sib, lm, 0.0), inv) for lm, inv in zip(lms, invs)]
        invs = [inv - mm(inv, tmp) for inv, tmp in zip(invs, tmps)]
        size *= 2
    return invs


def _delta_kernel(q_ref, k_ref, v_ref, *rest, c, n_h, n_z, precise):
    z_refs = rest[:n_z]
    gt_ref, gr_ref, nrm_ref, s0_ref, o_ref, sf_ref, st_ref = rest[n_z:]
    heads_per_z = n_h // n_z
    ci = pl.program_id(1)
    if precise:
        mm, mm_nt, mm_tn = _dot3, functools.partial(_dot3, dot=_dot_nt), functools.partial(_dot3, dot=_dot_tn)
    else:
        cast = lambda f: (lambda a, b: f(a.astype(BF16), b.astype(BF16)))
        mm, mm_nt, mm_tn = cast(_dot), cast(_dot_nt), cast(_dot_tn)
    mm_inv = mm
    ii = lax.broadcasted_iota(I32, (c, c), 0)
    jj = lax.broadcasted_iota(I32, (c, c), 1)
    incl = ii >= jj
    strict = ii > jj

    @pl.when(ci == 0)
    def _():
        st_ref[...] = s0_ref[...]

    gt = gt_ref[...]
    gth, gtl = _split_bf16(gt)
    tri = incl.astype(BF16)
    cum_t = _dot(tri, gth) + _dot(tri, gtl)
    grh, grl = _split_bf16(gr_ref[...])
    upper = (ii <= jj).astype(BF16)
    cum_r = _dot(grh, upper) + _dot(grl, upper)

    nrm = nrm_ref[...]
    heads = range(n_h)
    col = lambda h: pl.ds(h * HEAD_DIM, HEAD_DIM)
    qs = [q_ref[:, col(h)] for h in heads]
    ks = [k_ref[:, col(h)] for h in heads]
    vs = [v_ref[:, col(h)] for h in heads]
    s_old = [st_ref[h] for h in heads]
    betas = [gt[:, h:h + 1] for h in heads]
    cums = [cum_t[:, n_h + h:n_h + h + 1] for h in heads]
    lasts = [cum[c - 1:c, :] for cum in cums]
    decays = [jnp.exp(jnp.where(incl, cums[h] - cum_r[h:h + 1, :], NEG_INF)) for h in heads]
    ecums = [jnp.exp(cum) for cum in cums]
    kks = [mm_nt(k, k) for k in ks]
    qks = [mm_nt(q, k) for q, k in zip(qs, ks)]
    lowers = [jnp.where(strict, betas[h] * kks[h] * decays[h], 0.0) for h in heads]
    invs = _unit_lower_inverses(lowers, c, mm_inv)
    rhss = [jnp.concatenate([vs[h] * betas[h], ks[h] * (betas[h] * ecums[h])], axis=1) for h in heads]
    sols = [mm_inv(inv, rhs) for inv, rhs in zip(invs, rhss)]
    ws = [sols[h][:, :HEAD_DIM] - mm(sols[h][:, HEAD_DIM:], s_old[h]) for h in heads]
    outs = [mm(qs[h] * ecums[h], s_old[h]) + mm(qks[h] * decays[h], ws[h]) for h in heads]
    upds = [mm_tn(ks[h] * jnp.exp(lasts[h] - cums[h]), ws[h]) for h in heads]
    for h in heads:
        out = outs[h] * lax.rsqrt(jnp.mean(outs[h] * outs[h], axis=-1, keepdims=True) + RMS_EPS) * nrm
        z = z_refs[h // heads_per_z][:, pl.ds((h % heads_per_z) * HEAD_DIM, HEAD_DIM)]
        o_ref[:, col(h)] = out * _silu(z)
        st_ref[h] = s_old[h] * jnp.exp(lasts[h]) + upds[h]

    @pl.when(ci == pl.num_programs(1) - 1)
    def _():
        sf_ref[...] = st_ref[...]


def _delta_rule(dq, proj, zcol0, gates, delta_norm, s0, n_b, t, c, d_delta, precise):
    n_h = d_delta // HEAD_DIM
    nc = t // c
    gr = gates[:, n_h:2 * n_h].reshape(n_b, nc, c, n_h).transpose(0, 1, 3, 2)
    zw = math.gcd(zcol0, d_delta)
    rows = lambda off: pl.BlockSpec((c, d_delta), lambda b, i, off=off: (b * nc + i, off))
    zspecs = [pl.BlockSpec((c, zw), lambda b, i, j=j: (b * nc + i, zcol0 // zw + j)) for j in range(d_delta // zw)]
    state = pl.BlockSpec((None, n_h, HEAD_DIM, HEAD_DIM), lambda b, i: (b, 0, 0, 0))
    return pl.pallas_call(
        functools.partial(_delta_kernel, c=c, n_h=n_h, n_z=len(zspecs), precise=precise),
        grid=(n_b, nc),
        in_specs=[rows(0), rows(1), rows(2), *zspecs,
                  pl.BlockSpec((c, LANES), lambda b, i: (b * nc + i, 0)),
                  pl.BlockSpec((None, None, n_h, c), lambda b, i: (b, i, 0, 0)),
                  pl.BlockSpec((1, HEAD_DIM), lambda b, i: (0, 0)),
                  state],
        out_specs=[rows(0), state],
        out_shape=[jax.ShapeDtypeStruct((n_b * t, d_delta), F32),
                   jax.ShapeDtypeStruct((n_b, n_h, HEAD_DIM, HEAD_DIM), F32)],
        scratch_shapes=[pltpu.VMEM((n_h, HEAD_DIM, HEAD_DIM), F32)],
        name="delta_rule",
        compiler_params=_cparams(("parallel", "arbitrary"), 32),
    )(dq, dq, dq, *([proj] * len(zspecs)), gates, gr, delta_norm.reshape(1, -1), s0)


def _router_kernel(h_ref, g_ref, wr_ref, xn_ref, meta_ref, cnt_ref, *, tm):
    @pl.when(pl.program_id(0) == 0)
    def _():
        cnt_ref[...] = jnp.zeros(cnt_ref.shape, F32)

    x = h_ref[...]
    xn = x * lax.rsqrt(jnp.mean(x * x, axis=-1, keepdims=True) + RMS_EPS) * g_ref[...]
    for s in range(xn_ref.shape[1]):
        xn_ref[:, s, :] = xn[:, s * LANES:(s + 1) * LANES]
    logits = _dot3(xn, wr_ref[...])
    lane = lax.broadcasted_iota(I32, (tm, LANES), 1)
    big = jnp.int32(LANES)

    def first_max(vals, valid):
        vmax = jnp.max(jnp.where(valid, vals, -jnp.inf), axis=-1, keepdims=True)
        idx = jnp.min(jnp.where(valid & (vals == vmax), lane, big), axis=-1, keepdims=True)
        return vmax, idx

    gvalid = lane < N_GROUPS
    gmax, gstar = first_max(logits, gvalid)
    p_sel = 1.0 / jnp.sum(jnp.where(gvalid, jnp.exp(logits - gmax), 0.0), axis=-1, keepdims=True)
    e0 = N_GROUPS + gstar * EXPERTS_PER_GROUP
    evalid = (lane >= e0) & (lane < e0 + EXPERTS_PER_GROUP)
    emax, _ = first_max(logits, evalid)
    pe = jnp.where(evalid, jnp.exp(logits - emax), 0.0)
    pe = pe / jnp.sum(pe, axis=-1, keepdims=True)
    v1, i1 = first_max(pe, evalid)
    v2, i2 = first_max(pe, evalid & (lane != i1))
    den = v1 + v2
    g1 = p_sel * v1 / den
    g2 = p_sel * v2 / den
    ex1 = i1 - N_GROUPS
    ex2 = i2 - N_GROUPS
    oh1 = lane == ex1
    oh2 = lane == ex2
    oh = jnp.where(oh1 | oh2, 1.0, 0.0)
    ri = lax.broadcasted_iota(I32, (tm, tm), 0)
    rj = lax.broadcasted_iota(I32, (tm, tm), 1)
    before = _dot((ri > rj).astype(BF16), oh.astype(BF16)) + cnt_ref[...]
    r1 = jnp.sum(jnp.where(oh1, before, 0.0), axis=-1, keepdims=True)
    r2 = jnp.sum(jnp.where(oh2, before, 0.0), axis=-1, keepdims=True)
    cnt_ref[...] = cnt_ref[...] + jnp.sum(oh, axis=0, keepdims=True)
    meta = jnp.where(lane == 0, ex1.astype(F32),
           jnp.where(lane == 1, ex2.astype(F32),
           jnp.where(lane == 2, g1,
           jnp.where(lane == 3, g2,
           jnp.where(lane == 4, r1,
           jnp.where(lane == 5, r2, 0.0))))))
    meta_ref[...] = meta


def _router(h, gain, w_rg, w_re, tm):
    m, d = h.shape
    wr = jnp.zeros((d, LANES), F32)
    wr = wr.at[:, :N_GROUPS].set(w_rg)
    wr = wr.at[:, N_GROUPS:N_GROUPS + N_EXPERTS].set(w_re.transpose(1, 0, 2).reshape(d, N_EXPERTS))
    return pl.pallas_call(
        functools.partial(_router_kernel, tm=tm),
        grid=(m // tm,),
        in_specs=[pl.BlockSpec((tm, d), lambda i: (i, 0)),
                  pl.BlockSpec((1, d), lambda i: (0, 0)),
                  pl.BlockSpec((d, LANES), lambda i: (0, 0))],
        out_specs=[pl.BlockSpec((tm, d // LANES, LANES), lambda i: (i, 0, 0)),
                   pl.BlockSpec((tm, LANES), lambda i: (i, 0)),
                   pl.BlockSpec((1, LANES), lambda i: (0, 0))],
        out_shape=[jax.ShapeDtypeStruct((m, d // LANES, LANES), F32),
                   jax.ShapeDtypeStruct((m, LANES), F32),
                   jax.ShapeDtypeStruct((1, LANES), F32)],
        name="router",
        compiler_params=_cparams(("arbitrary",), 56),
    )(h, gain.reshape(1, d), wr)


def _row_copy(src_hbm, dst, sem, src_row, dst_row):
    return pltpu.make_async_copy(src_hbm.at[pl.ds(src_row, 1), :], dst.at[pl.ds(dst_row, 1), :], sem)


def _gather_kernel(tok_ref, used_ref, x_hbm, o_ref, buf_ref, sem, *, rows):
    i = pl.program_id(0)
    slot = i % 2

    def issue(step, into):
        base = step * rows

        @pl.when(base < used_ref[0])
        def _():
            def body(r, c):
                pltpu.make_async_copy(x_hbm.at[tok_ref[base + r]], buf_ref.at[into, r], sem.at[into]).start()
                return c
            lax.fori_loop(0, rows, body, 0)

    @pl.when(i == 0)
    def _():
        issue(0, 0)

    @pl.when(i + 1 < pl.num_programs(0))
    def _():
        issue(i + 1, 1 - slot)

    @pl.when(i * rows < used_ref[0])
    def _():
        pltpu.make_async_copy(x_hbm.at[pl.ds(0, rows)], buf_ref.at[slot], sem.at[slot]).wait()
        for s in range(buf_ref.shape[2]):
            o_ref[:, s * LANES:(s + 1) * LANES] = buf_ref[slot, :, s, :].astype(o_ref.dtype)


def _dispatch(xn, tok_of_row, used, n_rows, rows, out_dtype):
    slabs = xn.shape[1]
    d = slabs * LANES
    return pl.pallas_call(
        functools.partial(_gather_kernel, rows=rows),
        grid_spec=pltpu.PrefetchScalarGridSpec(
            num_scalar_prefetch=2,
            grid=(n_rows // rows,),
            in_specs=[pl.BlockSpec(memory_space=pl.ANY)],
            out_specs=pl.BlockSpec((rows, d), lambda i, tok, used: (i, 0)),
            scratch_shapes=[pltpu.VMEM((2, rows, slabs, LANES), F32), pltpu.SemaphoreType.DMA((2,))]),
        out_shape=jax.ShapeDtypeStruct((n_rows, d), out_dtype),
        name="dispatch",
        compiler_params=_cparams(("arbitrary",), 32),
    )(tok_of_row, used, xn)


def _for_valid_rows(nrows, total, sub, fn):
    need = (nrows + sub - 1) // sub
    for q in range(1, total // sub + 1):
        @pl.when(need == q)
        def _(q=q):
            fn(q * sub)


def _expert_up_kernel(be_ref, br_ref, bs_ref, x_ref, wg_ref, wu_ref, h_ref, *, precise):
    w = pl.program_id(1)

    def run(rows):
        x = x_ref[pl.ds(0, rows), :]
        if precise:
            hg = _dot3(x, wg_ref[...])
            hu = _dot3(x, wu_ref[...])
        else:
            hg = _dot(x, wg_ref[...].astype(BF16))
            hu = _dot(x, wu_ref[...].astype(BF16))
        h_ref[pl.ds(0, rows), :] = (_silu(hg) * hu).astype(h_ref.dtype)

    _for_valid_rows(br_ref[w], x_ref.shape[0], MOE_SUB, run)


def _expert_down_kernel(be_ref, br_ref, bs_ref, h_ref, wd_ref, y_ref, *, precise):
    w = pl.program_id(1)

    def run(rows):
        hid = h_ref[pl.ds(0, rows), :]
        if precise:
            y_ref[pl.ds(0, rows), :] = _dot3(hid, wd_ref[...])
        else:
            y_ref[pl.ds(0, rows), :] = _dot(hid, wd_ref[...].astype(BF16))

    _for_valid_rows(br_ref[w], h_ref.shape[0], MOE_SUB, run)


def _experts(xs, blk_e, blk_rows, blk_src, w_gate, w_up, w_down, layer, r, precise):
    n_rows, d = xs.shape
    nb = n_rows // r
    d_e = w_gate.shape[-1]
    tc, tn = MOE_HID_TILE, MOE_OUT_TILE
    hid = pl.pallas_call(
        functools.partial(_expert_up_kernel, precise=precise),
        grid_spec=pltpu.PrefetchScalarGridSpec(
            num_scalar_prefetch=3,
            grid=(d_e // tc, nb),
            in_specs=[pl.BlockSpec((r, d), lambda c, w, be, br, bs: (bs[w], 0)),
                      pl.BlockSpec((None, None, d, tc), lambda c, w, be, br, bs: (layer, be[w], 0, c)),
                      pl.BlockSpec((None, None, d, tc), lambda c, w, be, br, bs: (layer, be[w], 0, c))],
            out_specs=pl.BlockSpec((r, tc), lambda c, w, be, br, bs: (bs[w], c))),
        out_shape=jax.ShapeDtypeStruct((n_rows, d_e), F32 if precise else BF16),
        name="experts_up",
        compiler_params=_cparams(("arbitrary", "arbitrary"), 48),
    )(blk_e, blk_rows, blk_src, xs, w_gate, w_up)
    return pl.pallas_call(
        functools.partial(_expert_down_kernel, precise=precise),
        grid_spec=pltpu.PrefetchScalarGridSpec(
            num_scalar_prefetch=3,
            grid=(d // tn, nb),
            in_specs=[pl.BlockSpec((r, d_e), lambda n, w, be, br, bs: (bs[w], 0)),
                      pl.BlockSpec((None, None, d_e, tn), lambda n, w, be, br, bs: (layer, be[w], 0, n))],
            out_specs=pl.BlockSpec((r, tn), lambda n, w, be, br, bs: (bs[w], n))),
        out_shape=jax.ShapeDtypeStruct((n_rows, d), F32),
        name="experts_down",
        compiler_params=_cparams(("arbitrary", "arbitrary"), 48),
    )(blk_e, blk_rows, blk_src, hid, w_down)


def _combine_kernel(dest_ref, y_hbm, h_ref, meta_ref, gf_ref, o_ref, buf_ref, sem, *, rows, final):
    base = pl.program_id(0) * rows

    def issue(r, c):
        t2 = 2 * (base + r)
        _row_copy(y_hbm, buf_ref.at[0], sem, dest_ref[t2], r).start()
        _row_copy(y_hbm, buf_ref.at[1], sem, dest_ref[t2 + 1], r).start()
        return c
    lax.fori_loop(0, rows, issue, 0)
    pltpu.make_async_copy(y_hbm.at[pl.ds(0, rows), :], buf_ref.at[0], sem).wait()
    pltpu.make_async_copy(y_hbm.at[pl.ds(0, rows), :], buf_ref.at[1], sem).wait()
    meta = meta_ref[...]
    out = h_ref[...] + (buf_ref[0] * meta[:, 2:3] + buf_ref[1] * meta[:, 3:4])
    if final:
        out = out * lax.rsqrt(jnp.mean(out * out, axis=-1, keepdims=True) + RMS_EPS) * gf_ref[...]
    o_ref[...] = out


def _combine(y, dest, h, meta, norm_final, rows, final):
    m, d = h.shape
    return pl.pallas_call(
        functools.partial(_combine_kernel, rows=rows, final=final),
        grid_spec=pltpu.PrefetchScalarGridSpec(
            num_scalar_prefetch=1,
            grid=(m // rows,),
            in_specs=[pl.BlockSpec(memory_space=pl.ANY),
                      pl.BlockSpec((rows, d), lambda i, dest: (i, 0)),
                      pl.BlockSpec((rows, LANES), lambda i, dest: (i, 0)),
                      pl.BlockSpec((1, d), lambda i, dest: (0, 0))],
            out_specs=pl.BlockSpec((rows, d), lambda i, dest: (i, 0)),
            scratch_shapes=[pltpu.VMEM((2, rows, d), F32), pltpu.SemaphoreType.DMA(())]),
        out_shape=jax.ShapeDtypeStruct((m, d), F32),
        name="combine",
        compiler_params=_cparams(("arbitrary",), 48),
    )(dest, y, h, meta, norm_final.reshape(1, d))


def _moe(h, norm_ffn, w_rg, w_re, w_gate, w_up, w_down, norm_final, *, layer, final, precise, r, token_rows,
         gather_rows):
    m, d = h.shape
    xn, meta, counts = _router(h, norm_ffn, w_rg, w_re, token_rows)
    experts = meta[:, 0:2].astype(I32)
    rank = meta[:, 4:6].astype(I32)
    cnt = counts[0, :N_EXPERTS].astype(I32)
    padded = (cnt + r - 1) // r * r
    pad_end = jnp.cumsum(padded)
    pad_start = pad_end - padded
    dest = pad_start[experts] + rank
    nb = -(-(m * TOP_K) // r) + N_EXPERTS
    blk_start = jnp.arange(nb, dtype=I32) * r
    used = pad_end[-1]
    active = blk_start < used
    blk_e = jnp.minimum(jnp.sum(pad_end[None, :] <= blk_start[:, None], axis=1), N_EXPERTS - 1).astype(I32)
    blk_rows = jnp.where(active, jnp.clip(cnt[blk_e] - (blk_start - pad_start[blk_e]), 0, r), 0).astype(I32)
    last_blk = jnp.maximum(used // r - 1, 0)
    blk_e = jnp.where(active, blk_e, blk_e[last_blk])
    blk_src = jnp.minimum(jnp.arange(nb, dtype=I32), last_blk).astype(I32)
    tok_of_row = jnp.zeros((nb * r,), I32).at[dest.reshape(-1)].set(
        jnp.repeat(jnp.arange(m, dtype=I32), TOP_K))
    xs = _dispatch(xn, tok_of_row, used.reshape(1).astype(I32), nb * r, gather_rows, F32 if precise else BF16)
    y = _experts(xs, blk_e, blk_rows, blk_src, w_gate, w_up, w_down, layer, r, precise)
    return _combine(y, dest.reshape(-1).astype(I32), h, meta, norm_final, token_rows, final)


def kernel(x_prompt, x_sample, cache_k, cache_v, state_pool, state_conv, state_delta, page_table, meta, norm_mix, w_in, lambda_q1, lambda_k1, lambda_q2, lambda_k2, subln_gain, w_pool, pool_scale, conv_w, a_log, dt_bias, delta_norm, w_out, norm_ffn, w_router_group, w_router_expert, w_gate, w_up, w_down, norm_final):
    depth = w_in.shape[0]
    n_b, seq, d_model = x_prompt.shape
    n_s, ts, _ = x_sample.shape
    tp = N_META + seq
    tpad = SUBLANES
    n_mix = d_model // HEAD_DIM
    d_attn = (3 * n_mix) // 8 * HEAD_DIM
    d_delta = d_attn
    d_pool = d_model - d_attn - d_delta
    n_dh = d_delta // HEAD_DIM
    d_in = w_in.shape[-1]
    col_u = 3 * d_attn
    col_c = col_u + d_pool
    col_z = col_c + 3 * d_delta
    col_ba = col_z + d_delta
    d_in_pad = -(-d_in // PROJ_COL_TILE) * PROJ_COL_TILE
    mp = n_b * tp
    past = page_table.shape[1] * PAGE_SIZE
    slopes = _alibi_slopes(d_attn // HEAD_DIM)

    cache_kh = cache_k.transpose(0, 1, 3, 2, 4)
    cache_vh = cache_v.transpose(0, 1, 3, 2, 4)
    w_in_t = w_in.transpose(0, 2, 1)
    w_in_tb = w_in_t.astype(BF16)
    w_out_b = w_out.astype(BF16)

    hp = jnp.concatenate([jnp.broadcast_to(meta[None], (n_b, N_META, d_model)), x_prompt], axis=1)
    xp = hp.reshape(mp, d_model)
    xs = x_sample.reshape(n_s * ts, d_model)

    def pad_rows(a):
        a = a.reshape(n_s, ts, a.shape[-1])
        return jnp.pad(a, ((0, 0), (0, tpad - ts), (0, 0))).reshape(n_s * tpad, a.shape[-1])

    def real_rows(a):
        return a.reshape(n_s, tpad, -1)[:, :ts].reshape(n_s * ts, -1)

    def mixers(proj, oa, pool_prefix, conv_prefix, s0, n_seq, t, chunk, pos0, live, precise):
        ob = _pool_mix(proj, col_u, pool_prefix, w_pool[l], pool_scale[l], n_seq, t, pos0, precise)
        dq = _delta_prep(proj, col_c, conv_prefix, conv_w[l], n_seq, t, d_delta)
        gates = _gates(proj, col_ba, a_log[l], dt_bias[l], n_seq * t, t, n_dh)
        if live is not None:
            gates = jnp.where(live, gates, 0.0)
        oc, sf = _delta_rule(dq, proj, col_z, gates, delta_norm[l], s0, n_seq, t, chunk, d_delta, precise)
        return jnp.concatenate([oa, ob, oc], axis=1), sf

    moe_p = dict(precise=False, r=MOE_ROWS, token_rows=PROMPT_TOKEN_ROWS, gather_rows=GATHER_ROWS)
    moe_s = dict(precise=True, r=MOE_SUB, token_rows=n_s * ts, gather_rows=MOE_SUB)

    outs = {k: [] for k in ('kp', 'vp', 'pp', 'cp', 'dp', 'ks', 'vs', 'ps', 'cs', 'ds')}
    for l in range(depth):
        lw = {'lambda_q1': lambda_q1[l], 'lambda_k1': lambda_k1[l], 'lambda_q2': lambda_q2[l],
              'lambda_k2': lambda_k2[l], 'subln_gain': subln_gain[l]}
        lam_init = 0.8 - 0.6 * math.exp(-0.3 * l)
        final = l == depth - 1
        moe_w = (norm_ffn[l], w_router_group[l], w_router_expert[l], w_gate, w_up, w_down, norm_final)

        proj = _dense(xp, w_in_tb, gain=norm_mix[l], tm=PROMPT_ROW_TILE, layer=l, w_transposed=True)
        oa_p, k_p, v_p = _prompt_attention(proj, lw, lam_init, jnp.asarray(slopes), n_b, tp, d_attn)
        act, sf_p = mixers(
            proj, oa_p, jnp.zeros((n_b, 16, d_pool), F32), jnp.zeros((n_b, 8, 3 * d_delta), F32),
            jnp.zeros((n_b, n_dh, HEAD_DIM, HEAD_DIM), F32), n_b, tp, DELTA_CHUNK, 0, None, False)
        h = _dense(act, w_out_b, res=xp, tm=PROMPT_ROW_TILE, layer=l)
        if not final:
            xp = _moe(h, *moe_w, layer=l, final=False, **moe_p)

        proj_s32 = _dense(xs, w_in_t, gain=norm_mix[l], tm=n_s * ts, precise=True, layer=l,
                          w_transposed=True)
        proj_s3 = proj_s32.reshape(n_s, ts, d_in_pad)
        proj_s = pad_rows(proj_s32)
        pre_s = jnp.concatenate([jnp.zeros((n_s, 16 - POOL_BUF, d_pool), F32), state_pool[l]], axis=1)
        cpre_s = jnp.concatenate([jnp.zeros((n_s, 8 - (CONV_W - 1), 3 * d_delta), F32), state_conv[l]], axis=1)
        live = (jnp.arange(n_s * tpad) % tpad < ts)[:, None]
        oa_s = _paged_attention(proj_s, cache_kh, cache_vh, l, page_table, lw, lam_init, slopes, ts, tpad, d_attn)
        act_s, sf_s = mixers(proj_s, oa_s, pre_s, cpre_s, state_delta[l], n_s, tpad, tpad, past, live, True)
        h_s = _dense(real_rows(act_s), w_out, res=xs, tm=n_s * ts, precise=True, layer=l)
        if not final:
            xs = _moe(h_s, *moe_w, layer=l, final=False, **moe_s)
        else:
            both = _moe(jnp.concatenate([h, h_s], axis=0), *moe_w, layer=l, final=True,
                        **dict(moe_p, token_rows=ALL_TOKEN_ROWS))
            xp, xs = both[:mp], both[mp:]

        pp3 = proj.reshape(n_b, tp, d_in_pad)
        n_ah = d_attn // HEAD_DIM
        outs['kp'].append(k_p.transpose(0, 2, 1, 3))
        outs['vp'].append(v_p.transpose(0, 2, 1, 3))
        outs['pp'].append(pp3[:, tp - POOL_BUF:, col_u:col_u + d_pool])
        outs['cp'].append(pp3[:, tp - (CONV_W - 1):, col_c:col_c + 3 * d_delta])
        outs['dp'].append(sf_p)
        outs['ks'].append(proj_s3[:, :, d_attn:2 * d_attn].reshape(n_s, ts, n_ah, HEAD_DIM))
        outs['vs'].append(proj_s3[:, :, 2 * d_attn:3 * d_attn].reshape(n_s, ts, n_ah, HEAD_DIM))
        u_s = proj_s3[:, :, col_u:col_u + d_pool]
        outs['ps'].append(jnp.concatenate([state_pool[l], u_s], axis=1)[:, -POOL_BUF:])
        c_s = proj_s3[:, :, col_c:col_c + 3 * d_delta]
        outs['cs'].append(jnp.concatenate([state_conv[l], c_s], axis=1)[:, -(CONV_W - 1):])
        outs['ds'].append(sf_s)

    y_prompt = xp.reshape(n_b, tp, d_model)[:, N_META:]
    y_sample = xs.reshape(n_s, ts, d_model)
    st = lambda k: jnp.stack(outs[k])
    return (y_prompt, y_sample, st('kp'), st('vp'), st('pp'), st('cp'), st('dp'),
            st('ks'), st('vs'), st('ps'), st('cs'), st('ds'))
```

```python
import functools
import math

import numpy as np
import jax
import jax.numpy as jnp
from jax import lax
from jax.experimental import pallas as pl
from jax.experimental.pallas import tpu as pltpu

F32 = jnp.float32
BF16 = jnp.bfloat16
I32 = jnp.int32

HEAD_DIM = 128
QK_HALF = HEAD_DIM // 2
POOL_WINDOWS = (2, 4, 8, 16)
POOL_BUF = 15
CONV_W = 4
N_META = 16
N_GROUPS = 4
EXPERTS_PER_GROUP = 8
N_EXPERTS = N_GROUPS * EXPERTS_PER_GROUP
TOP_K = 2
PAGE_SIZE = 128
RMS_EPS = 1e-6
NEG_INF = -1e30

LANES = 128
SUBLANES = 8
MIB = 1024 * 1024

PROMPT_ROW_TILE = 688
PROMPT_TOKEN_ROWS = 344
ALL_TOKEN_ROWS = 296
PROJ_COL_TILE = 512
ATTN_BLOCK = 384
DELTA_CHUNK = 48
DELTA_BASE = 8
PAGES_PER_STEP = 4
PAGED_HEAD_GROUP = 12
MOE_ROWS = 512
MOE_SUB = 128
MOE_HID_TILE = 256
MOE_OUT_TILE = 1024
GATHER_ROWS = 256


def _cparams(semantics, vmem_mib):
    return pltpu.CompilerParams(dimension_semantics=semantics, vmem_limit_bytes=vmem_mib * MIB)


def _alibi_slopes(n_heads):
    def pow2_slopes(m):
        start = 2.0 ** (-8.0 / m)
        return [start ** (i + 1) for i in range(m)]
    closest = 2 ** int(math.floor(math.log2(n_heads)))
    slopes = pow2_slopes(closest) + pow2_slopes(2 * closest)[0::2][: n_heads - closest]
    return np.array(slopes, dtype=np.float32)


def _sigmoid(x):
    return 1.0 / (1.0 + jnp.exp(-x))


def _silu(x):
    return x * _sigmoid(x)


def _split_bf16(x):
    hi = x.astype(BF16)
    lo = (x - hi.astype(F32)).astype(BF16)
    return hi, lo


def _stack_hilo(x):
    hi = x.astype(BF16).astype(F32)
    return jnp.concatenate([hi, x - hi], axis=0).astype(BF16)


def _dot(a, b):
    return jnp.dot(a, b, preferred_element_type=F32)


def _dot_nt(a, b):
    return lax.dot_general(a, b, (((1,), (1,)), ((), ())), preferred_element_type=F32)


def _dot_tn(a, b):
    return lax.dot_general(a, b, (((0,), (0,)), ((), ())), preferred_element_type=F32)


def _dot3(a, b, dot=_dot):
    ah, al = _split_bf16(a)
    bh, bl = _split_bf16(b)
    return dot(ah, bh) + (dot(ah, bl) + dot(al, bh))


def _proj_kernel(*refs, normed, residual, precise, w_transposed):
    mm = _dot_nt if w_transposed else _dot
    it = iter(refs)
    x_ref = next(it)
    g_ref = next(it) if normed else None
    w_ref = next(it)
    r_ref = next(it) if residual else None
    o_ref = next(it)
    xh_ref = next(it)
    xl_ref = next(it) if precise else None

    @pl.when(pl.program_id(1) == 0)
    def _():
        x = x_ref[...]
        if normed:
            ms = jnp.mean(x * x, axis=-1, keepdims=True)
            x = x * lax.rsqrt(ms + RMS_EPS) * g_ref[...]
        if precise:
            xh_ref[...], xl_ref[...] = _split_bf16(x)
        else:
            xh_ref[...] = x.astype(BF16)

    if precise:
        wh, wl = _split_bf16(w_ref[...])
        acc = mm(xh_ref[...], wh) + (mm(xh_ref[...], wl) + mm(xl_ref[...], wh))
    else:
        acc = mm(xh_ref[...], w_ref[...])
    if residual:
        acc = r_ref[...] + acc
    o_ref[...] = acc


def _dense(x, w, gain=None, res=None, *, tm, tn=PROJ_COL_TILE, precise=False, layer=None, w_transposed=False):
    m, d = x.shape
    n = w.shape[-2] if w_transposed else w.shape[-1]
    n_out = pl.cdiv(n, tn) * tn
    assert m % tm == 0
    in_specs = [pl.BlockSpec((tm, d), lambda i, j: (i, 0))]
    args = [x]
    if gain is not None:
        in_specs.append(pl.BlockSpec((1, d), lambda i, j: (0, 0)))
        args.append(gain.reshape(1, d))
    if layer is None:
        in_specs.append(pl.BlockSpec((d, tn), lambda i, j: (0, j)))
    elif w_transposed:
        in_specs.append(pl.BlockSpec((None, tn, d), lambda i, j: (layer, j, 0)))
    else:
        in_specs.append(pl.BlockSpec((None, d, tn), lambda i, j: (layer, 0, j)))
    args.append(w)
    if res is not None:
        in_specs.append(pl.BlockSpec((tm, tn), lambda i, j: (i, j)))
        args.append(res)
    return pl.pallas_call(
        functools.partial(_proj_kernel, normed=gain is not None, residual=res is not None, precise=precise,
                          w_transposed=w_transposed),
        grid=(m // tm, n_out // tn),
        in_specs=in_specs,
        out_specs=pl.BlockSpec((tm, tn), lambda i, j: (i, j)),
        out_shape=jax.ShapeDtypeStruct((m, n_out), F32),
        scratch_shapes=[pltpu.VMEM((tm, d), BF16)] * (2 if precise else 1),
        name="dense_precise" if precise else "dense",
        compiler_params=_cparams(("parallel", "arbitrary"), 56),
    )(*args)


def _lambda_value(lq1, lk1, lq2, lk2, lam_init):
    a = jnp.sum(lq1[...] * lk1[...], axis=-1, keepdims=True)
    b = jnp.sum(lq2[...] * lk2[...], axis=-1, keepdims=True)
    return jnp.exp(a) - jnp.exp(b) + lam_init


def _subln(o, gain, lam_init):
    o = o * lax.rsqrt(jnp.mean(o * o, axis=-1, keepdims=True) + RMS_EPS) * gain
    return o * (1.0 - lam_init)


def _prompt_attn_kernel(slopes_ref, lq1, lk1, lq2, lk2, gain_ref, q_ref, k_ref, v_ref, o_ref, ko_ref, vo_ref,
                        qp_ref, kp_ref, vp_ref, op_ref, *, t, tb, lam_init):
    h = pl.program_id(1)
    ko_ref[...] = k_ref[...]
    vo_ref[...] = v_ref[...]
    slope = slopes_ref[h]
    lam = _lambda_value(lq1, lk1, lq2, lk2, lam_init)
    tp = qp_ref.shape[0]
    nb = tp // tb
    for src, dst in ((q_ref, qp_ref), (k_ref, kp_ref), (v_ref, vp_ref)):
        dst[pl.ds(0, t), :] = src[...]
        dst[pl.ds(t, tp - t), :] = jnp.zeros((tp - t, HEAD_DIM), F32)
    dloc = lax.broadcasted_iota(I32, (tb, tb), 0) - lax.broadcasted_iota(I32, (tb, tb), 1)

    def qblock(qi, carry):
        r0 = pl.multiple_of(qi * tb, tb)
        q = qp_ref[pl.ds(r0, tb), :] * (QK_HALF ** -0.5)
        q1 = q[:, :QK_HALF].astype(BF16)
        q2 = q[:, QK_HALF:].astype(BF16)

        def kvblock(kj, c):
            m1, l1, a1, m2, l2, a2 = c
            c0 = pl.multiple_of(kj * tb, tb)
            kk = kp_ref[pl.ds(c0, tb), :]
            vv = vp_ref[pl.ds(c0, tb), :].astype(BF16)
            dist = dloc + (r0 - c0)
            causal = dist >= 0
            bias = slope * dist.astype(F32)

            s1 = _dot_nt(q1, kk[:, :QK_HALF].astype(BF16))
            s2 = _dot_nt(q2, kk[:, QK_HALF:].astype(BF16))
            s1 = jnp.where(causal, s1 - bias, NEG_INF)
            s2 = jnp.where(causal, s2 - bias, NEG_INF)
            n1 = jnp.maximum(m1, jnp.max(s1, axis=-1, keepdims=True))
            n2 = jnp.maximum(m2, jnp.max(s2, axis=-1, keepdims=True))
            p1 = jnp.exp(s1 - n1)
            p2 = jnp.exp(s2 - n2)
            al1 = jnp.exp(m1 - n1)
            al2 = jnp.exp(m2 - n2)
            pv1 = _dot(p1.astype(BF16), vv)
            pv2 = _dot(p2.astype(BF16), vv)
            l1 = al1 * l1 + jnp.sum(p1, axis=-1, keepdims=True)
            l2 = al2 * l2 + jnp.sum(p2, axis=-1, keepdims=True)
            return n1, l1, al1 * a1 + pv1, n2, l2, al2 * a2 + pv2

        m0 = jnp.full((tb, 1), NEG_INF, F32)
        z1 = jnp.zeros((tb, 1), F32)
        za = jnp.zeros((tb, HEAD_DIM), F32)
        m1, l1, a1, m2, l2, a2 = lax.fori_loop(0, qi + 1, kvblock, (m0, z1, za, m0, z1, za))
        o = a1 / l1 - lam * (a2 / l2)
        op_ref[pl.ds(r0, tb), :] = _subln(o, gain_ref[...], lam_init)
        return carry

    lax.fori_loop(0, nb, qblock, 0)
    o_ref[...] = op_ref[pl.ds(0, t), :]


def _prompt_attention(proj, lw, lam_init, slopes, n_b, t, d_attn):
    n_h = d_attn // HEAD_DIM
    tb = ATTN_BLOCK
    tp = -(-t // tb) * tb
    vec = lambda n: pl.BlockSpec((1, n), lambda b, h: (0, 0))
    blk = lambda off: pl.BlockSpec((t, HEAD_DIM), lambda b, h, off=off: (b, off + h))
    return pl.pallas_call(
        functools.partial(_prompt_attn_kernel, t=t, tb=tb, lam_init=lam_init),
        grid=(n_b, n_h),
        in_specs=[pl.BlockSpec(memory_space=pltpu.SMEM),
                  vec(QK_HALF), vec(QK_HALF), vec(QK_HALF), vec(QK_HALF), vec(HEAD_DIM),
                  blk(0), blk(n_h), blk(2 * n_h)],
        out_specs=[pl.BlockSpec((t, HEAD_DIM), lambda b, h: (b, h)),
                   pl.BlockSpec((None, None, t, HEAD_DIM), lambda b, h: (b, h, 0, 0)),
                   pl.BlockSpec((None, None, t, HEAD_DIM), lambda b, h: (b, h, 0, 0))],
        out_shape=[jax.ShapeDtypeStruct((n_b * t, d_attn), F32),
                   jax.ShapeDtypeStruct((n_b, n_h, t, HEAD_DIM), F32),
                   jax.ShapeDtypeStruct((n_b, n_h, t, HEAD_DIM), F32)],
        scratch_shapes=[pltpu.VMEM((tp, HEAD_DIM), F32)] * 4,
        name="prompt_attention",
        compiler_params=_cparams(("parallel", "parallel"), 40),
    )(slopes, lw['lambda_q1'].reshape(1, -1), lw['lambda_k1'].reshape(1, -1),
      lw['lambda_q2'].reshape(1, -1), lw['lambda_k2'].reshape(1, -1), lw['subln_gain'].reshape(1, -1),
      proj, proj, proj)


def _paged_attn_kernel(pt_ref, lq1, lk1, lq2, lk2, gain_ref, qkv_ref, *rest, ts, n_h, n_pages, pps, lam_init,
                       slopes):
    k_refs, v_refs = rest[:pps], rest[pps:2 * pps]
    o_ref, qm_ref, m_ref, l_ref, acc_ref = rest[2 * pps:]
    p = pl.program_id(1)
    d_attn = n_h * HEAD_DIM
    past = n_pages * PAGE_SIZE
    width = pps * PAGE_SIZE
    rows2 = 2 * ts

    @pl.when(p == 0)
    def _():
        lane = lax.broadcasted_iota(I32, (rows2, HEAD_DIM), 1)
        srow = lax.broadcasted_iota(I32, (rows2, HEAD_DIM), 0)
        for h in range(n_h):
            q = qkv_ref[:, pl.ds(h * HEAD_DIM, HEAD_DIM)] * (QK_HALF ** -0.5)
            qa = jnp.where(lane < QK_HALF, q, 0.0)
            qb = jnp.where(lane >= QK_HALF, pltpu.roll(q, ts, 0), 0.0)
            qm_ref[h] = jnp.where(srow < ts, qa, qb)
        m_ref[...] = jnp.full(m_ref.shape, NEG_INF, F32)
        l_ref[...] = jnp.zeros(l_ref.shape, F32)
        acc_ref[...] = jnp.zeros(acc_ref.shape, F32)

    def stacked3(lhs, rhs, dot):
        rh, rl = _split_bf16(rhs)
        both = dot(_stack_hilo(lhs), rh)
        return both[:rows2] + (both[rows2:] + dot(lhs.astype(BF16), rl))

    def softmax_step(h, s, v, m_old, l_old, acc_old):
        mn = jnp.maximum(m_old, jnp.max(s, axis=-1, keepdims=True))
        pr = jnp.exp(s - mn)
        al = jnp.exp(m_old - mn)
        return mn, al * l_old + jnp.sum(pr, axis=-1, keepdims=True), al * acc_old + stacked3(pr, v, _dot)

    row = lax.broadcasted_iota(I32, (rows2, width), 0)
    col = lax.broadcasted_iota(I32, (rows2, width), 1)
    tok = jnp.where(row >= ts, row - ts, row)
    dist = (past - p * width) + tok - col
    distf = dist.astype(F32)
    cat = lambda refs, h: jnp.concatenate([refs[i][h] for i in range(pps)], axis=0)
    for h0 in range(0, n_h, PAGED_HEAD_GROUP):
        heads = range(h0, min(h0 + PAGED_HEAD_GROUP, n_h))
        ksplit = {h: _split_bf16(cat(k_refs, h)) for h in heads}
        qstack = {h: _stack_hilo(qm_ref[h]) for h in heads}
        both = {h: _dot_nt(qstack[h], ksplit[h][0]) for h in heads}
        cross = {h: _dot_nt(qm_ref[h].astype(BF16), ksplit[h][1]) for h in heads}
        ss = {h: both[h][:rows2] + (both[h][rows2:] + cross[h]) - float(slopes[h]) * distf for h in heads}
        m_old = {h: m_ref[h] for h in heads}
        m_new = {h: jnp.maximum(m_old[h], jnp.max(ss[h], axis=-1, keepdims=True)) for h in heads}
        prs = {h: jnp.exp(ss[h] - m_new[h]) for h in heads}
        als = {h: jnp.exp(m_old[h] - m_new[h]) for h in heads}
        vsplit = {h: _split_bf16(cat(v_refs, h)) for h in heads}
        pstack = {h: _stack_hilo(prs[h]) for h in heads}
        pboth = {h: _dot(pstack[h], vsplit[h][0]) for h in heads}
        pcross = {h: _dot(prs[h].astype(BF16), vsplit[h][1]) for h in heads}
        for h in heads:
            l_ref[h] = als[h] * l_ref[h] + jnp.sum(prs[h], axis=-1, keepdims=True)
            acc_ref[h] = als[h] * acc_ref[h] + (pboth[h][:rows2] + (pboth[h][rows2:] + pcross[h]))
            m_ref[h] = m_new[h]

    def update(h, s, v):
        m_ref[h], l_ref[h], acc_ref[h] = softmax_step(h, s, v, m_ref[h], l_ref[h], acc_ref[h])

    @pl.when(p == n_pages // pps - 1)
    def _():
        lam = _lambda_value(lq1, lk1, lq2, lk2, lam_init)
        rown = lax.broadcasted_iota(I32, (rows2, rows2), 0)
        coln = lax.broadcasted_iota(I32, (rows2, rows2), 1)
        tokn = jnp.where(rown >= ts, rown - ts, rown)
        distn = tokn - coln
        valid = (distn >= 0) & (coln < ts)
        for h in range(n_h):
            kn = qkv_ref[:, pl.ds(d_attn + h * HEAD_DIM, HEAD_DIM)]
            vn = qkv_ref[:, pl.ds(2 * d_attn + h * HEAD_DIM, HEAD_DIM)]
            s = stacked3(qm_ref[h], kn, _dot_nt)
            s = jnp.where(valid, s - float(slopes[h]) * distn.astype(F32), NEG_INF)
            update(h, s, vn)
            o = acc_ref[h] / l_ref[h]
            o = o - lam * pltpu.roll(o, ts, 0)
            o_ref[:, pl.ds(h * HEAD_DIM, HEAD_DIM)] = _subln(o, gain_ref[...], lam_init)


def _paged_attention(proj_s, cache_k, cache_v, layer, page_table, lw, lam_init, slopes, ts, tpad, d_attn):
    n_seq, n_pages = page_table.shape
    n_h = d_attn // HEAD_DIM
    pps = PAGES_PER_STEP
    assert n_pages % pps == 0
    vec = lambda n: pl.BlockSpec((1, n), lambda s, p, pt: (0, 0))
    cache_specs = [pl.BlockSpec((None, None, n_h, PAGE_SIZE, HEAD_DIM),
                                lambda s, p, pt, i=i: (layer, pt[s * n_pages + p * pps + i], 0, 0, 0))
                   for i in range(pps)]
    return pl.pallas_call(
        functools.partial(_paged_attn_kernel, ts=ts, n_h=n_h, n_pages=n_pages, pps=pps, lam_init=lam_init,
                          slopes=tuple(float(x) for x in slopes)),
        grid_spec=pltpu.PrefetchScalarGridSpec(
            num_scalar_prefetch=1,
            grid=(n_seq, n_pages // pps),
            in_specs=[vec(QK_HALF), vec(QK_HALF), vec(QK_HALF), vec(QK_HALF), vec(HEAD_DIM),
                      pl.BlockSpec((tpad, 3 * d_attn), lambda s, p, pt: (s, 0)),
                      *cache_specs, *cache_specs],
            out_specs=pl.BlockSpec((tpad, d_attn), lambda s, p, pt: (s, 0)),
            scratch_shapes=[pltpu.VMEM((n_h, 2 * ts, HEAD_DIM), F32),
                            pltpu.VMEM((n_h, 2 * ts, 1), F32),
                            pltpu.VMEM((n_h, 2 * ts, 1), F32),
                            pltpu.VMEM((n_h, 2 * ts, HEAD_DIM), F32)]),
        out_shape=jax.ShapeDtypeStruct((n_seq * tpad, d_attn), F32),
        name="paged_attention",
        compiler_params=_cparams(("parallel", "arbitrary"), 48),
    )(page_table.reshape(-1), lw['lambda_q1'].reshape(1, -1), lw['lambda_k1'].reshape(1, -1),
      lw['lambda_q2'].reshape(1, -1), lw['lambda_k2'].reshape(1, -1), lw['subln_gain'].reshape(1, -1),
      proj_s, *([cache_k] * pps), *([cache_v] * pps))


def _pool_kernel(u_ref, pre_ref, w_ref, sc_ref, o_ref, buf_ref, *, t, pos0, precise):
    g = pl.program_id(1)
    hist = pre_ref.shape[0]
    buf_ref[pl.ds(0, hist), :] = pre_ref[...]
    buf_ref[pl.ds(hist, t), :] = u_ref[...]
    pos = pos0 + 1 + lax.broadcasted_iota(I32, (t, 1), 0)
    for gi, w in enumerate(POOL_WINDOWS):
        @pl.when(g == gi)
        def _(w=w):
            acc = buf_ref[pl.ds(hist, t), :]
            for i in range(1, w):
                acc = acc + buf_ref[pl.ds(hist - i, t), :]
            cnt = jnp.minimum(pos, w).astype(F32)
            diff = acc / cnt - u_ref[...]
            if precise:
                mixed = _dot3(diff, w_ref[...])
            else:
                mixed = _dot(diff.astype(BF16), w_ref[...].astype(BF16))
            o_ref[...] = mixed * sc_ref[...]


def _pool_mix(proj, col0, prefix16, w_pool, scale, n_b, t, pos0, precise):
    n_g = len(POOL_WINDOWS)
    grp = w_pool.shape[-1]
    cb = col0 // grp
    return pl.pallas_call(
        functools.partial(_pool_kernel, t=t, pos0=pos0, precise=precise),
        grid=(n_b, n_g),
        in_specs=[pl.BlockSpec((t, grp), lambda b, g: (b, cb + g)),
                  pl.BlockSpec((None, 16, grp), lambda b, g: (b, 0, g)),
                  pl.BlockSpec((None, grp, grp), lambda b, g: (g, 0, 0)),
                  pl.BlockSpec((1, grp), lambda b, g: (0, g))],
        out_specs=pl.BlockSpec((t, grp), lambda b, g: (b, g)),
        out_shape=jax.ShapeDtypeStruct((n_b * t, n_g * grp), F32),
        scratch_shapes=[pltpu.VMEM((16 + t, grp), F32)],
        name="pool_mix",
        compiler_params=_cparams(("parallel", "parallel"), 32),
    )(proj, prefix16, w_pool, scale.reshape(1, -1))


def _dprep_kernel(x_ref, pre_ref, cw_ref, o_ref, buf_ref, *, t, n_h):
    j = pl.program_id(1)
    buf_ref[pl.ds(0, 8), :] = pre_ref[...]
    buf_ref[pl.ds(8, t), :] = x_ref[...]
    acc = buf_ref[pl.ds(8 - (CONV_W - 1), t), :] * cw_ref[pl.ds(0, 1), :]
    for i in range(1, CONV_W):
        acc = acc + buf_ref[pl.ds(8 - (CONV_W - 1) + i, t), :] * cw_ref[pl.ds(i, 1), :]
    s = _silu(acc)
    nrm = s * lax.rsqrt(jnp.sum(s * s, axis=-1, keepdims=True) + RMS_EPS)
    o_ref[...] = jnp.where(j < n_h, nrm * (HEAD_DIM ** -0.5), jnp.where(j < 2 * n_h, nrm, s))


def _delta_prep(proj, col0, prefix8, conv_w, n_b, t, d_delta):
    n_h = d_delta // HEAD_DIM
    cb = col0 // HEAD_DIM
    return pl.pallas_call(
        functools.partial(_dprep_kernel, t=t, n_h=n_h),
        grid=(n_b, 3 * n_h),
        in_specs=[pl.BlockSpec((t, HEAD_DIM), lambda b, j: (b, cb + j)),
                  pl.BlockSpec((None, 8, HEAD_DIM), lambda b, j: (b, 0, j)),
                  pl.BlockSpec((CONV_W, HEAD_DIM), lambda b, j: (0, j))],
        out_specs=pl.BlockSpec((t, HEAD_DIM), lambda b, j: (b, j)),
        out_shape=jax.ShapeDtypeStruct((n_b * t, 3 * d_delta), F32),
        scratch_shapes=[pltpu.VMEM((8 + t, HEAD_DIM), F32)],
        name="delta_prep",
        compiler_params=_cparams(("parallel", "parallel"), 32),
    )(proj, prefix8, conv_w)


def _gates_kernel(x_ref, alog_ref, dtb_ref, o_ref, *, n_h):
    x = x_ref[...]
    lane = lax.broadcasted_iota(I32, x.shape, 1)
    beta = _sigmoid(x)
    z = x + dtb_ref[...]
    softplus = jnp.maximum(z, 0.0) + jnp.log(1.0 + jnp.exp(-jnp.abs(z)))
    g = -jnp.exp(alog_ref[...]) * softplus
    o_ref[...] = jnp.where(lane < n_h, beta, g)


def _gates(proj, col0, a_log, dt_bias, rows, tm, n_h):
    cb = col0 // LANES
    pad = lambda v: jnp.zeros((1, LANES), F32).at[0, n_h:2 * n_h].set(v.astype(F32))
    return pl.pallas_call(
        functools.partial(_gates_kernel, n_h=n_h),
        grid=(rows // tm,),
        in_specs=[pl.BlockSpec((tm, LANES), lambda i: (i, cb)),
                  pl.BlockSpec((1, LANES), lambda i: (0, 0)),
                  pl.BlockSpec((1, LANES), lambda i: (0, 0))],
        out_specs=pl.BlockSpec((tm, LANES), lambda i: (i, 0)),
        out_shape=jax.ShapeDtypeStruct((rows, LANES), F32),
        name="delta_gates",
        compiler_params=_cparams(("parallel",), 32),
    )(proj, pad(a_log), pad(dt_bias))


def _unit_lower_inverses(lms, c, mm):
    ii = lax.broadcasted_iota(I32, (c, c), 0)
    jj = lax.broadcasted_iota(I32, (c, c), 1)
    eye = (ii == jj).astype(F32)
    base = DELTA_BASE
    shift = int(math.log2(base))
    same = (ii >> shift) == (jj >> shift)
    npows = [-jnp.where(same, lm, 0.0) for lm in lms]
    invs = [eye + n0 for n0 in npows]
    for _ in range(shift - 1):
        npows = [mm(n, n) for n in npows]
        invs = [inv + mm(inv, n) for inv, n in zip(invs, npows)]
    size = base
    while size < c:
        s = int(math.log2(size))
        sib = ((ii >> (s + 1)) == (jj >> (s + 1))) & ((ii >> s) != (jj >> s))
        tmps = [mm(jnp.where(sib, lm, 0.0), inv) for lm, inv in zip(lms, invs)]
        invs = [inv - mm(inv, tmp) for inv, tmp in zip(invs, tmps)]
        size *= 2
    return invs


def _delta_kernel(q_ref, k_ref, v_ref, *rest, c, n_h, n_z, precise):
    z_refs = rest[:n_z]
    gt_ref, gr_ref, nrm_ref, s0_ref, o_ref, sf_ref, st_ref = rest[n_z:]
    heads_per_z = n_h // n_z
    ci = pl.program_id(1)
    if precise:
        mm, mm_nt, mm_tn = _dot3, functools.partial(_dot3, dot=_dot_nt), functools.partial(_dot3, dot=_dot_tn)
    else:
        cast = lambda f: (lambda a, b: f(a.astype(BF16), b.astype(BF16)))
        mm, mm_nt, mm_tn = cast(_dot), cast(_dot_nt), cast(_dot_tn)
    mm_inv = mm
    ii = lax.broadcasted_iota(I32, (c, c), 0)
    jj = lax.broadcasted_iota(I32, (c, c), 1)
    incl = ii >= jj
    strict = ii > jj

    @pl.when(ci == 0)
    def _():
        st_ref[...] = s0_ref[...]

    gt = gt_ref[...]
    gth, gtl = _split_bf16(gt)
    tri = incl.astype(BF16)
    cum_t = _dot(tri, gth) + _dot(tri, gtl)
    grh, grl = _split_bf16(gr_ref[...])
    upper = (ii <= jj).astype(BF16)
    cum_r = _dot(grh, upper) + _dot(grl, upper)

    nrm = nrm_ref[...]
    heads = range(n_h)
    col = lambda h: pl.ds(h * HEAD_DIM, HEAD_DIM)
    qs = [q_ref[:, col(h)] for h in heads]
    ks = [k_ref[:, col(h)] for h in heads]
    vs = [v_ref[:, col(h)] for h in heads]
    s_old = [st_ref[h] for h in heads]
    betas = [gt[:, h:h + 1] for h in heads]
    cums = [cum_t[:, n_h + h:n_h + h + 1] for h in heads]
    lasts = [cum[c - 1:c, :] for cum in cums]
    decays = [jnp.exp(jnp.where(incl, cums[h] - cum_r[h:h + 1, :], NEG_INF)) for h in heads]
    ecums = [jnp.exp(cum) for cum in cums]
    kks = [mm_nt(k, k) for k in ks]
    qks = [mm_nt(q, k) for q, k in zip(qs, ks)]
    lowers = [jnp.where(strict, betas[h] * kks[h] * decays[h], 0.0) for h in heads]
    invs = _unit_lower_inverses(lowers, c, mm_inv)
    rhss = [jnp.concatenate([vs[h] * betas[h], ks[h] * (betas[h] * ecums[h])], axis=1) for h in heads]
    sols = [mm_inv(inv, rhs) for inv, rhs in zip(invs, rhss)]
    ws = [sols[h][:, :HEAD_DIM] - mm(sols[h][:, HEAD_DIM:], s_old[h]) for h in heads]
    outs = [mm(qs[h] * ecums[h], s_old[h]) + mm(qks[h] * decays[h], ws[h]) for h in heads]
    upds = [mm_tn(ks[h] * jnp.exp(lasts[h] - cums[h]), ws[h]) for h in heads]
    for h in heads:
        out = outs[h] * lax.rsqrt(jnp.mean(outs[h] * outs[h], axis=-1, keepdims=True) + RMS_EPS) * nrm
        z = z_refs[h // heads_per_z][:, pl.ds((h % heads_per_z) * HEAD_DIM, HEAD_DIM)]
        o_ref[:, col(h)] = out * _silu(z)
        st_ref[h] = s_old[h] * jnp.exp(lasts[h]) + upds[h]

    @pl.when(ci == pl.num_programs(1) - 1)
    def _():
        sf_ref[...] = st_ref[...]


def _delta_rule(dq, proj, zcol0, gates, delta_norm, s0, n_b, t, c, d_delta, precise):
    n_h = d_delta // HEAD_DIM
    nc = t // c
    gr = gates[:, n_h:2 * n_h].reshape(n_b, nc, c, n_h).transpose(0, 1, 3, 2)
    zw = math.gcd(zcol0, d_delta)
    rows = lambda off: pl.BlockSpec((c, d_delta), lambda b, i, off=off: (b * nc + i, off))
    zspecs = [pl.BlockSpec((c, zw), lambda b, i, j=j: (b * nc + i, zcol0 // zw + j)) for j in range(d_delta // zw)]
    state = pl.BlockSpec((None, n_h, HEAD_DIM, HEAD_DIM), lambda b, i: (b, 0, 0, 0))
    return pl.pallas_call(
        functools.partial(_delta_kernel, c=c, n_h=n_h, n_z=len(zspecs), precise=precise),
        grid=(n_b, nc),
        in_specs=[rows(0), rows(1), rows(2), *zspecs,
                  pl.BlockSpec((c, LANES), lambda b, i: (b * nc + i, 0)),
                  pl.BlockSpec((None, None, n_h, c), lambda b, i: (b, i, 0, 0)),
                  pl.BlockSpec((1, HEAD_DIM), lambda b, i: (0, 0)),
                  state],
        out_specs=[rows(0), state],
        out_shape=[jax.ShapeDtypeStruct((n_b * t, d_delta), F32),
                   jax.ShapeDtypeStruct((n_b, n_h, HEAD_DIM, HEAD_DIM), F32)],
        scratch_shapes=[pltpu.VMEM((n_h, HEAD_DIM, HEAD_DIM), F32)],
        name="delta_rule",
        compiler_params=_cparams(("parallel", "arbitrary"), 32),
    )(dq, dq, dq, *([proj] * len(zspecs)), gates, gr, delta_norm.reshape(1, -1), s0)


def _router_kernel(h_ref, g_ref, wr_ref, xn_ref, meta_ref, cnt_ref, *, tm):
    @pl.when(pl.program_id(0) == 0)
    def _():
        cnt_ref[...] = jnp.zeros(cnt_ref.shape, F32)

    x = h_ref[...]
    xn = x * lax.rsqrt(jnp.mean(x * x, axis=-1, keepdims=True) + RMS_EPS) * g_ref[...]
    for s in range(xn_ref.shape[1]):
        xn_ref[:, s, :] = xn[:, s * LANES:(s + 1) * LANES]
    logits = _dot3(xn, wr_ref[...])
    lane = lax.broadcasted_iota(I32, (tm, LANES), 1)
    big = jnp.int32(LANES)

    def first_max(vals, valid):
        vmax = jnp.max(jnp.where(valid, vals, -jnp.inf), axis=-1, keepdims=True)
        idx = jnp.min(jnp.where(valid & (vals == vmax), lane, big), axis=-1, keepdims=True)
        return vmax, idx

    gvalid = lane < N_GROUPS
    gmax, gstar = first_max(logits, gvalid)
    p_sel = 1.0 / jnp.sum(jnp.where(gvalid, jnp.exp(logits - gmax), 0.0), axis=-1, keepdims=True)
    e0 = N_GROUPS + gstar * EXPERTS_PER_GROUP
    evalid = (lane >= e0) & (lane < e0 + EXPERTS_PER_GROUP)
    emax, _ = first_max(logits, evalid)
    pe = jnp.where(evalid, jnp.exp(logits - emax), 0.0)
    pe = pe / jnp.sum(pe, axis=-1, keepdims=True)
    v1, i1 = first_max(pe, evalid)
    v2, i2 = first_max(pe, evalid & (lane != i1))
    den = v1 + v2
    g1 = p_sel * v1 / den
    g2 = p_sel * v2 / den
    ex1 = i1 - N_GROUPS
    ex2 = i2 - N_GROUPS
    oh1 = lane == ex1
    oh2 = lane == ex2
    oh = jnp.where(oh1 | oh2, 1.0, 0.0)
    ri = lax.broadcasted_iota(I32, (tm, tm), 0)
    rj = lax.broadcasted_iota(I32, (tm, tm), 1)
    before = _dot((ri > rj).astype(BF16), oh.astype(BF16)) + cnt_ref[...]
    r1 = jnp.sum(jnp.where(oh1, before, 0.0), axis=-1, keepdims=True)
    r2 = jnp.sum(jnp.where(oh2, before, 0.0), axis=-1, keepdims=True)
    cnt_ref[...] = cnt_ref[...] + jnp.sum(oh, axis=0, keepdims=True)
    meta = jnp.where(lane == 0, ex1.astype(F32),
           jnp.where(lane == 1, ex2.astype(F32),
           jnp.where(lane == 2, g1,
           jnp.where(lane == 3, g2,
           jnp.where(lane == 4, r1,
           jnp.where(lane == 5, r2, 0.0))))))
    meta_ref[...] = meta


def _router(h, gain, w_rg, w_re, tm):
    m, d = h.shape
    wr = jnp.zeros((d, LANES), F32)
    wr = wr.at[:, :N_GROUPS].set(w_rg)
    wr = wr.at[:, N_GROUPS:N_GROUPS + N_EXPERTS].set(w_re.transpose(1, 0, 2).reshape(d, N_EXPERTS))
    return pl.pallas_call(
        functools.partial(_router_kernel, tm=tm),
        grid=(m // tm,),
        in_specs=[pl.BlockSpec((tm, d), lambda i: (i, 0)),
                  pl.BlockSpec((1, d), lambda i: (0, 0)),
                  pl.BlockSpec((d, LANES), lambda i: (0, 0))],
        out_specs=[pl.BlockSpec((tm, d // LANES, LANES), lambda i: (i, 0, 0)),
                   pl.BlockSpec((tm, LANES), lambda i: (i, 0)),
                   pl.BlockSpec((1, LANES), lambda i: (0, 0))],
        out_shape=[jax.ShapeDtypeStruct((m, d // LANES, LANES), F32),
                   jax.ShapeDtypeStruct((m, LANES), F32),
                   jax.ShapeDtypeStruct((1, LANES), F32)],
        name="router",
        compiler_params=_cparams(("arbitrary",), 56),
    )(h, gain.reshape(1, d), wr)


def _row_copy(src_hbm, dst, sem, src_row, dst_row):
    return pltpu.make_async_copy(src_hbm.at[pl.ds(src_row, 1), :], dst.at[pl.ds(dst_row, 1), :], sem)


def _gather_kernel(tok_ref, used_ref, x_hbm, o_ref, buf_ref, sem, *, rows):
    i = pl.program_id(0)
    slot = i % 2

    def issue(step, into):
        base = step * rows

        @pl.when(base < used_ref[0])
        def _():
            def body(r, c):
                pltpu.make_async_copy(x_hbm.at[tok_ref[base + r]], buf_ref.at[into, r], sem.at[into]).start()
                return c
            lax.fori_loop(0, rows, body, 0)

    @pl.when(i == 0)
    def _():
        issue(0, 0)

    @pl.when(i + 1 < pl.num_programs(0))
    def _():
        issue(i + 1, 1 - slot)

    @pl.when(i * rows < used_ref[0])
    def _():
        pltpu.make_async_copy(x_hbm.at[pl.ds(0, rows)], buf_ref.at[slot], sem.at[slot]).wait()
        for s in range(buf_ref.shape[2]):
            o_ref[:, s * LANES:(s + 1) * LANES] = buf_ref[slot, :, s, :].astype(o_ref.dtype)


def _dispatch(xn, tok_of_row, used, n_rows, rows, out_dtype):
    slabs = xn.shape[1]
    d = slabs * LANES
    return pl.pallas_call(
        functools.partial(_gather_kernel, rows=rows),
        grid_spec=pltpu.PrefetchScalarGridSpec(
            num_scalar_prefetch=2,
            grid=(n_rows // rows,),
            in_specs=[pl.BlockSpec(memory_space=pl.ANY)],
            out_specs=pl.BlockSpec((rows, d), lambda i, tok, used: (i, 0)),
            scratch_shapes=[pltpu.VMEM((2, rows, slabs, LANES), F32), pltpu.SemaphoreType.DMA((2,))]),
        out_shape=jax.ShapeDtypeStruct((n_rows, d), out_dtype),
        name="dispatch",
        compiler_params=_cparams(("arbitrary",), 32),
    )(tok_of_row, used, xn)


def _for_valid_rows(nrows, total, sub, fn):
    need = (nrows + sub - 1) // sub
    for q in range(1, total // sub + 1):
        @pl.when(need == q)
        def _(q=q):
            fn(q * sub)


def _expert_up_kernel(be_ref, br_ref, bs_ref, x_ref, wg_ref, wu_ref, h_ref, *, precise):
    w = pl.program_id(1)

    def run(rows):
        x = x_ref[pl.ds(0, rows), :]
        if precise:
            hg = _dot3(x, wg_ref[...])
            hu = _dot3(x, wu_ref[...])
        else:
            hg = _dot(x, wg_ref[...].astype(BF16))
            hu = _dot(x, wu_ref[...].astype(BF16))
        h_ref[pl.ds(0, rows), :] = (_silu(hg) * hu).astype(h_ref.dtype)

    _for_valid_rows(br_ref[w], x_ref.shape[0], MOE_SUB, run)


def _expert_down_kernel(be_ref, br_ref, bs_ref, h_ref, wd_ref, y_ref, *, precise):
    w = pl.program_id(1)

    def run(rows):
        hid = h_ref[pl.ds(0, rows), :]
        if precise:
            y_ref[pl.ds(0, rows), :] = _dot3(hid, wd_ref[...])
        else:
            y_ref[pl.ds(0, rows), :] = _dot(hid, wd_ref[...].astype(BF16))

    _for_valid_rows(br_ref[w], h_ref.shape[0], MOE_SUB, run)


def _experts(xs, blk_e, blk_rows, blk_src, w_gate, w_up, w_down, layer, r, precise):
    n_rows, d = xs.shape
    nb = n_rows // r
    d_e = w_gate.shape[-1]
    tc, tn = MOE_HID_TILE, MOE_OUT_TILE
    hid = pl.pallas_call(
        functools.partial(_expert_up_kernel, precise=precise),
        grid_spec=pltpu.PrefetchScalarGridSpec(
            num_scalar_prefetch=3,
            grid=(d_e // tc, nb),
            in_specs=[pl.BlockSpec((r, d), lambda c, w, be, br, bs: (bs[w], 0)),
                      pl.BlockSpec((None, None, d, tc), lambda c, w, be, br, bs: (layer, be[w], 0, c)),
                      pl.BlockSpec((None, None, d, tc), lambda c, w, be, br, bs: (layer, be[w], 0, c))],
            out_specs=pl.BlockSpec((r, tc), lambda c, w, be, br, bs: (bs[w], c))),
        out_shape=jax.ShapeDtypeStruct((n_rows, d_e), F32 if precise else BF16),
        name="experts_up",
        compiler_params=_cparams(("arbitrary", "arbitrary"), 48),
    )(blk_e, blk_rows, blk_src, xs, w_gate, w_up)
    return pl.pallas_call(
        functools.partial(_expert_down_kernel, precise=precise),
        grid_spec=pltpu.PrefetchScalarGridSpec(
            num_scalar_prefetch=3,
            grid=(d // tn, nb),
            in_specs=[pl.BlockSpec((r, d_e), lambda n, w, be, br, bs: (bs[w], 0)),
                      pl.BlockSpec((None, None, d_e, tn), lambda n, w, be, br, bs: (layer, be[w], 0, n))],
            out_specs=pl.BlockSpec((r, tn), lambda n, w, be, br, bs: (bs[w], n))),
        out_shape=jax.ShapeDtypeStruct((n_rows, d), F32),
        name="experts_down",
        compiler_params=_cparams(("arbitrary", "arbitrary"), 48),
    )(blk_e, blk_rows, blk_src, hid, w_down)


def _combine_kernel(dest_ref, y_hbm, h_ref, meta_ref, gf_ref, o_ref, buf_ref, sem, *, rows, final):
    base = pl.program_id(0) * rows

    def issue(r, c):
        t2 = 2 * (base + r)
        _row_copy(y_hbm, buf_ref.at[0], sem, dest_ref[t2], r).start()
        _row_copy(y_hbm, buf_ref.at[1], sem, dest_ref[t2 + 1], r).start()
        return c
    lax.fori_loop(0, rows, issue, 0)
    pltpu.make_async_copy(y_hbm.at[pl.ds(0, rows), :], buf_ref.at[0], sem).wait()
    pltpu.make_async_copy(y_hbm.at[pl.ds(0, rows), :], buf_ref.at[1], sem).wait()
    meta = meta_ref[...]
    out = h_ref[...] + (buf_ref[0] * meta[:, 2:3] + buf_ref[1] * meta[:, 3:4])
    if final:
        out = out * lax.rsqrt(jnp.mean(out * out, axis=-1, keepdims=True) + RMS_EPS) * gf_ref[...]
    o_ref[...] = out


def _combine(y, dest, h, meta, norm_final, rows, final):
    m, d = h.shape
    return pl.pallas_call(
        functools.partial(_combine_kernel, rows=rows, final=final),
        grid_spec=pltpu.PrefetchScalarGridSpec(
            num_scalar_prefetch=1,
            grid=(m // rows,),
            in_specs=[pl.BlockSpec(memory_space=pl.ANY),
                      pl.BlockSpec((rows, d), lambda i, dest: (i, 0)),
                      pl.BlockSpec((rows, LANES), lambda i, dest: (i, 0)),
                      pl.BlockSpec((1, d), lambda i, dest: (0, 0))],
            out_specs=pl.BlockSpec((rows, d), lambda i, dest: (i, 0)),
            scratch_shapes=[pltpu.VMEM((2, rows, d), F32), pltpu.SemaphoreType.DMA(())]),
        out_shape=jax.ShapeDtypeStruct((m, d), F32),
        name="combine",
        compiler_params=_cparams(("arbitrary",), 48),
    )(dest, y, h, meta, norm_final.reshape(1, d))


def _moe(h, norm_ffn, w_rg, w_re, w_gate, w_up, w_down, norm_final, *, layer, final, precise, r, token_rows,
         gather_rows):
    m, d = h.shape
    xn, meta, counts = _router(h, norm_ffn, w_rg, w_re, token_rows)
    experts = meta[:, 0:2].astype(I32)
    rank = meta[:, 4:6].astype(I32)
    cnt = counts[0, :N_EXPERTS].astype(I32)
    padded = (cnt + r - 1) // r * r
    pad_end = jnp.cumsum(padded)
    pad_start = pad_end - padded
    dest = pad_start[experts] + rank
    nb = -(-(m * TOP_K) // r) + N_EXPERTS
    blk_start = jnp.arange(nb, dtype=I32) * r
    used = pad_end[-1]
    active = blk_start < used
    blk_e = jnp.minimum(jnp.sum(pad_end[None, :] <= blk_start[:, None], axis=1), N_EXPERTS - 1).astype(I32)
    blk_rows = jnp.where(active, jnp.clip(cnt[blk_e] - (blk_start - pad_start[blk_e]), 0, r), 0).astype(I32)
    last_blk = jnp.maximum(used // r - 1, 0)
    blk_e = jnp.where(active, blk_e, blk_e[last_blk])
    blk_src = jnp.minimum(jnp.arange(nb, dtype=I32), last_blk).astype(I32)
    tok_of_row = jnp.zeros((nb * r,), I32).at[dest.reshape(-1)].set(
        jnp.repeat(jnp.arange(m, dtype=I32), TOP_K))
    xs = _dispatch(xn, tok_of_row, used.reshape(1).astype(I32), nb * r, gather_rows, F32 if precise else BF16)
    y = _experts(xs, blk_e, blk_rows, blk_src, w_gate, w_up, w_down, layer, r, precise)
    return _combine(y, dest.reshape(-1).astype(I32), h, meta, norm_final, token_rows, final)


def kernel(x_prompt, x_sample, cache_k, cache_v, state_pool, state_conv, state_delta, page_table, meta, norm_mix, w_in, lambda_q1, lambda_k1, lambda_q2, lambda_k2, subln_gain, w_pool, pool_scale, conv_w, a_log, dt_bias, delta_norm, w_out, norm_ffn, w_router_group, w_router_expert, w_gate, w_up, w_down, norm_final):
    depth = w_in.shape[0]
    n_b, seq, d_model = x_prompt.shape
    n_s, ts, _ = x_sample.shape
    tp = N_META + seq
    tpad = SUBLANES
    n_mix = d_model // HEAD_DIM
    d_attn = (3 * n_mix) // 8 * HEAD_DIM
    d_delta = d_attn
    d_pool = d_model - d_attn - d_delta
    n_dh = d_delta // HEAD_DIM
    d_in = w_in.shape[-1]
    col_u = 3 * d_attn
    col_c = col_u + d_pool
    col_z = col_c + 3 * d_delta
    col_ba = col_z + d_delta
    d_in_pad = -(-d_in // PROJ_COL_TILE) * PROJ_COL_TILE
    mp = n_b * tp
    past = page_table.shape[1] * PAGE_SIZE
    slopes = _alibi_slopes(d_attn // HEAD_DIM)

    cache_kh = cache_k.transpose(0, 1, 3, 2, 4)
    cache_vh = cache_v.transpose(0, 1, 3, 2, 4)
    w_in_t = w_in.transpose(0, 2, 1)
    w_in_tb = w_in_t.astype(BF16)
    w_out_b = w_out.astype(BF16)

    hp = jnp.concatenate([jnp.broadcast_to(meta[None], (n_b, N_META, d_model)), x_prompt], axis=1)
    xp = hp.reshape(mp, d_model)
    xs = x_sample.reshape(n_s * ts, d_model)

    def pad_rows(a):
        a = a.reshape(n_s, ts, a.shape[-1])
        return jnp.pad(a, ((0, 0), (0, tpad - ts), (0, 0))).reshape(n_s * tpad, a.shape[-1])

    def real_rows(a):
        return a.reshape(n_s, tpad, -1)[:, :ts].reshape(n_s * ts, -1)

    def mixers(proj, oa, pool_prefix, conv_prefix, s0, n_seq, t, chunk, pos0, live, precise):
        ob = _pool_mix(proj, col_u, pool_prefix, w_pool[l], pool_scale[l], n_seq, t, pos0, precise)
        dq = _delta_prep(proj, col_c, conv_prefix, conv_w[l], n_seq, t, d_delta)
        gates = _gates(proj, col_ba, a_log[l], dt_bias[l], n_seq * t, t, n_dh)
        if live is not None:
            gates = jnp.where(live, gates, 0.0)
        oc, sf = _delta_rule(dq, proj, col_z, gates, delta_norm[l], s0, n_seq, t, chunk, d_delta, precise)
        return jnp.concatenate([oa, ob, oc], axis=1), sf

    moe_p = dict(precise=False, r=MOE_ROWS, token_rows=PROMPT_TOKEN_ROWS, gather_rows=GATHER_ROWS)
    moe_s = dict(precise=True, r=MOE_SUB, token_rows=n_s * ts, gather_rows=MOE_SUB)

    outs = {k: [] for k in ('kp', 'vp', 'pp', 'cp', 'dp', 'ks', 'vs', 'ps', 'cs', 'ds')}
    for l in range(depth):
        lw = {'lambda_q1': lambda_q1[l], 'lambda_k1': lambda_k1[l], 'lambda_q2': lambda_q2[l],
              'lambda_k2': lambda_k2[l], 'subln_gain': subln_gain[l]}
        lam_init = 0.8 - 0.6 * math.exp(-0.3 * l)
        final = l == depth - 1
        moe_w = (norm_ffn[l], w_router_group[l], w_router_expert[l], w_gate, w_up, w_down, norm_final)

        proj = _dense(xp, w_in_tb, gain=norm_mix[l], tm=PROMPT_ROW_TILE, layer=l, w_transposed=True)
        oa_p, k_p, v_p = _prompt_attention(proj, lw, lam_init, jnp.asarray(slopes), n_b, tp, d_attn)
        act, sf_p = mixers(
            proj, oa_p, jnp.zeros((n_b, 16, d_pool), F32), jnp.zeros((n_b, 8, 3 * d_delta), F32),
            jnp.zeros((n_b, n_dh, HEAD_DIM, HEAD_DIM), F32), n_b, tp, DELTA_CHUNK, 0, None, False)
        h = _dense(act, w_out_b, res=xp, tm=PROMPT_ROW_TILE, layer=l)
        if not final:
            xp = _moe(h, *moe_w, layer=l, final=False, **moe_p)

        proj_s32 = _dense(xs, w_in_t, gain=norm_mix[l], tm=n_s * ts, precise=True, layer=l,
                          w_transposed=True)
        proj_s3 = proj_s32.reshape(n_s, ts, d_in_pad)
        proj_s = pad_rows(proj_s32)
        pre_s = jnp.concatenate([jnp.zeros((n_s, 16 - POOL_BUF, d_pool), F32), state_pool[l]], axis=1)
        cpre_s = jnp.concatenate([jnp.zeros((n_s, 8 - (CONV_W - 1), 3 * d_delta), F32), state_conv[l]], axis=1)
        live = (jnp.arange(n_s * tpad) % tpad < ts)[:, None]
        oa_s = _paged_attention(proj_s, cache_kh, cache_vh, l, page_table, lw, lam_init, slopes, ts, tpad, d_attn)
        act_s, sf_s = mixers(proj_s, oa_s, pre_s, cpre_s, state_delta[l], n_s, tpad, tpad, past, live, True)
        h_s = _dense(real_rows(act_s), w_out, res=xs, tm=n_s * ts, precise=True, layer=l)
        if not final:
            xs = _moe(h_s, *moe_w, layer=l, final=False, **moe_s)
        else:
            both = _moe(jnp.concatenate([h, h_s], axis=0), *moe_w, layer=l, final=True,
                        **dict(moe_p, token_rows=ALL_TOKEN_ROWS))
            xp, xs = both[:mp], both[mp:]

        pp3 = proj.reshape(n_b, tp, d_in_pad)
        n_ah = d_attn // HEAD_DIM
        outs['kp'].append(k_p.transpose(0, 2, 1, 3))
        outs['vp'].append(v_p.transpose(0, 2, 1, 3))
        outs['pp'].append(pp3[:, tp - POOL_BUF:, col_u:col_u + d_pool])
        outs['cp'].append(pp3[:, tp - (CONV_W - 1):, col_c:col_c + 3 * d_delta])
        outs['dp'].append(sf_p)
        outs['ks'].append(proj_s3[:, :, d_attn:2 * d_attn].reshape(n_s, ts, n_ah, HEAD_DIM))
        outs['vs'].append(proj_s3[:, :, 2 * d_attn:3 * d_attn].reshape(n_s, ts, n_ah, HEAD_DIM))
        u_s = proj_s3[:, :, col_u:col_u + d_pool]
        outs['ps'].append(jnp.concatenate([state_pool[l], u_s], axis=1)[:, -POOL_BUF:])
        c_s = proj_s3[:, :, col_c:col_c + 3 * d_delta]
        outs['cs'].append(jnp.concatenate([state_conv[l], c_s], axis=1)[:, -(CONV_W - 1):])
        outs['ds'].append(sf_s)

    y_prompt = xp.reshape(n_b, tp, d_model)[:, N_META:]
    y_sample = xs.reshape(n_s, ts, d_model)
    st = lambda k: jnp.stack(outs[k])
    return (y_prompt, y_sample, st('kp'), st('vp'), st('pp'), st('cp'), st('dp'),
            st('ks'), st('vs'), st('ps'), st('cs'), st('ds'))
```
